```python
import jax, jax.numpy as jnp
from jax import lax
import numpy as np

D_MODEL = 1024
BATCH = 8
SEQ = 8192
DEPTH = 2

CTX_LEN = 256
GRID_W = 64
HG_HEADS = 4
HG_DK = 128
HG_DV = 128
HG_WIDTH = HG_HEADS * HG_DK
POOL_WINDOWS = (2, 4, 8, 16)
POOL_GROUPS = len(POOL_WINDOWS)
POOL_WIDTH = 512
POOL_GD = POOL_WIDTH // POOL_GROUPS
D_FF = 2816
CHUNK = 64
N_MOD = 9
EPS = 1e-6
IN_SIZES = (HG_WIDTH, HG_WIDTH, HG_WIDTH, HG_HEADS * HG_DV, HG_HEADS * HG_DV, POOL_WIDTH, D_MODEL, D_MODEL)
IN_COLS = sum(IN_SIZES)
IN_SPLITS = tuple(int(s) for s in np.cumsum(IN_SIZES)[:-1])

kernel_name = 'hgrn2_pool_macaron_hybrid'


def rmsnorm(x, w):
    xf = x.astype(jnp.float32)
    y = xf * lax.rsqrt(jnp.mean(xf * xf, axis=-1, keepdims=True) + EPS)
    return (y * w.astype(jnp.float32)).astype(x.dtype)


def modulate(x, shift, scale):
    return x * (1 + scale) + shift


def ffn_half(h, mod, k0, norm_w, wg, wu, wd):
    u = modulate(rmsnorm(h, norm_w), mod[:, k0, None], mod[:, k0 + 1, None])
    y = (jax.nn.silu(u @ wg) * (u @ wu)) @ wd
    return h + 0.5 * mod[:, k0 + 2, None] * y


def gla_scan(q, k, v, log_f, s0):
    b, n, h, _ = q.shape
    nc = n // CHUNK

    def to_chunks(t):
        return jnp.moveaxis(t.astype(jnp.float32).reshape(b, nc, CHUNK, h, t.shape[-1]), 1, 0)

    mask = jnp.tril(jnp.ones((CHUNK, CHUNK), dtype=bool))

    def step(state, inp):
        qc, kc, vc, lf = inp
        cum = jnp.cumsum(lf, axis=1)
        inter = jnp.einsum('bchd,bhde->bche', qc * jnp.exp(cum), state)
        diff = cum[:, :, None] - cum[:, None]
        decay = jnp.exp(jnp.where(mask[None, :, :, None, None], diff, -jnp.inf))
        attn = jnp.einsum('btshd,bshd->bhts', qc[:, :, None] * decay, kc)
        intra = jnp.einsum('bhts,bshe->bthe', attn, vc)
        last = cum[:, -1]
        k_dec = kc * jnp.exp(last[:, None] - cum)
        new_state = jnp.exp(last)[..., None] * state + jnp.einsum('bshd,bshe->bhde', k_dec, vc)
        return new_state, inter + intra

    s_fin, out = lax.scan(step, s0, (to_chunks(q), to_chunks(k), to_chunks(v), to_chunks(log_f)))
    out = jnp.moveaxis(out, 0, 1).reshape(b, n, h, v.shape[-1])
    return out, s_fin


def hgrn2_bidir(q, i, zf, zb, lb_f, lb_b, s0_f, s0_b):
    b, n, _ = q.shape

    def heads(t):
        return t.astype(jnp.float32).reshape(b, n, HG_HEADS, -1)

    qh = jax.nn.silu(heads(q))
    vh = heads(i)

    def gates(z, lb):
        z = heads(z)
        lb = lb.astype(jnp.float32).reshape(HG_HEADS, HG_DK)
        f = lb + (1 - lb) * jax.nn.sigmoid(z)
        return (1 - lb) * jax.nn.sigmoid(-z), jnp.log(f)

    kf, lff = gates(zf, lb_f)
    kb, lfb = gates(zb, lb_b)
    o_f, s_f = gla_scan(qh, kf, vh, lff, s0_f)
    flip = lambda t: jnp.flip(t, axis=1)
    o_b, s_b = gla_scan(flip(qh), flip(kb), flip(vh), flip(lfb), s0_b)
    return o_f + flip(o_b), s_f, s_b


def box_mean(x, axis, w):
    length = x.shape[axis]
    cs = jnp.cumsum(x, axis=axis)
    cs = jnp.concatenate([jnp.zeros_like(lax.slice_in_dim(cs, 0, 1, axis=axis)), cs], axis=axis)
    idx = jnp.arange(length)
    lo = jnp.clip(idx - w // 2, 0, length)
    hi = jnp.clip(idx + w - w // 2, 0, length)
    s = jnp.take(cs, hi, axis=axis) - jnp.take(cs, lo, axis=axis)
    shape = [1] * x.ndim
    shape[axis] = length
    return s / (hi - lo).astype(jnp.float32).reshape(shape)


def pool_mixer(v, pool_w, pool_scale, on_grid):
    b, n, _ = v.shape
    vf = v.astype(jnp.float32)
    if on_grid:
        rows = n // GRID_W
        vf = vf.reshape(b, rows, GRID_W, POOL_WIDTH)
    outs = []
    for gi, w in enumerate(POOL_WINDOWS):
        seg = vf[..., gi * POOL_GD:(gi + 1) * POOL_GD]
        m = box_mean(box_mean(seg, 2, w), 1, w) if on_grid else box_mean(seg, 1, w)
        outs.append(m - seg)
    d = jnp.stack(outs, axis=-2).reshape(b, n, POOL_GROUPS, POOL_GD)
    y = jnp.einsum('bngc,gcd->bngd', d, pool_w.astype(jnp.float32)).reshape(b, n, POOL_WIDTH)
    return (y * pool_scale.astype(jnp.float32)).astype(v.dtype)


def mixer(u, w_in, lb_f, lb_b, hg_norm_w, pool_w, pool_scale, w_a, w_b, w_out, s0_f, s0_b, on_grid, need_out):
    proj = u @ w_in
    q, zf, zb, i, g, pv, ga, gb = jnp.split(proj, IN_SPLITS, axis=-1)
    o, s_f, s_b = hgrn2_bidir(q, i, zf, zb, lb_f, lb_b, s0_f, s0_b)
    if not need_out:
        return None, s_f, s_b
    b, n = u.shape[0], u.shape[1]
    on = o * lax.rsqrt(jnp.mean(o * o, axis=-1, keepdims=True) + EPS)
    on = on * hg_norm_w.astype(jnp.float32).reshape(HG_HEADS, HG_DV)
    a = (on.reshape(b, n, HG_HEADS * HG_DV) * jax.nn.silu(g.astype(jnp.float32))).astype(u.dtype)
    p = pool_mixer(pv, pool_w, pool_scale, on_grid)
    merged = jax.nn.sigmoid(ga) * (a @ w_a) + jax.nn.sigmoid(gb) * (p @ w_b)
    return merged @ w_out, s_f, s_b


def setup_inputs(seed: int = 0) -> dict:
    key = jax.random.key(seed)
    ks = jax.random.split(key, 19)
    nrm = lambda k, shape, s: jax.random.normal(k, shape, jnp.float32) * s
    return {
        'x': nrm(ks[0], (BATCH, SEQ, D_MODEL), 1.0),
        'c': nrm(ks[1], (BATCH, D_MODEL), 1.0),
        'ctx': nrm(ks[2], (BATCH, CTX_LEN, D_MODEL), 1.0),
        'c_ctx': nrm(ks[3], (D_MODEL,), 1.0),
        'ada_w': nrm(ks[4], (DEPTH, D_MODEL, N_MOD * D_MODEL), D_MODEL ** -0.5),
        'ada_b': nrm(ks[5], (DEPTH, N_MOD * D_MODEL), 0.02),
        'norm_w': 1.0 + nrm(ks[6], (DEPTH, 3, D_MODEL), 0.02),
        'ffn_wg': nrm(ks[7], (DEPTH, 2, D_MODEL, D_FF), D_MODEL ** -0.5),
        'ffn_wu': nrm(ks[8], (DEPTH, 2, D_MODEL, D_FF), D_MODEL ** -0.5),
        'ffn_wd': nrm(ks[9], (DEPTH, 2, D_FF, D_MODEL), D_FF ** -0.5),
        'w_in': nrm(ks[10], (DEPTH, D_MODEL, IN_COLS), D_MODEL ** -0.5),
        'lower_bounds': nrm(ks[11], (2, DEPTH, HG_WIDTH), 0.5),
        'hg_norm_w': 1.0 + nrm(ks[12], (DEPTH, HG_HEADS * HG_DV), 0.02),
        'pool_w': nrm(ks[13], (DEPTH, POOL_GROUPS, POOL_GD, POOL_GD), POOL_GD ** -0.5),
        'pool_scale': 1.0 + nrm(ks[14], (DEPTH, POOL_WIDTH), 0.02),
        'w_branch_a': nrm(ks[15], (DEPTH, HG_HEADS * HG_DV, D_MODEL), (HG_HEADS * HG_DV) ** -0.5),
        'w_branch_b': nrm(ks[16], (DEPTH, POOL_WIDTH, D_MODEL), POOL_WIDTH ** -0.5),
        'w_out': nrm(ks[17], (DEPTH, D_MODEL, D_MODEL), D_MODEL ** -0.5),
        'final_norm_w': 1.0 + nrm(ks[18], (D_MODEL,), 0.02),
    }


def reference(x, c, ctx, c_ctx, ada_w, ada_b, norm_w, ffn_wg, ffn_wu, ffn_wd, w_in, lower_bounds,
              hg_norm_w, pool_w, pool_scale, w_branch_a, w_branch_b, w_out, final_norm_w):
    b = x.shape[0]
    h, hc = x, ctx
    silu_c = jax.nn.silu(c)
    silu_cc = jax.nn.silu(c_ctx)[None]
    sm = jax.nn.softmax(lower_bounds.astype(jnp.float32), axis=1)
    lbs = jnp.cumsum(sm, axis=1) - sm[:, :1]
    zeros_state = jnp.zeros((b, HG_HEADS, HG_DK, HG_DV), jnp.float32)
    for l in range(DEPTH):
        last = l == DEPTH - 1
        mod_lat = (silu_c @ ada_w[l] + ada_b[l]).reshape(b, N_MOD, D_MODEL)
        mod_ctx = (silu_cc @ ada_w[l] + ada_b[l]).reshape(1, N_MOD, D_MODEL)
        mix_args = (w_in[l], lbs[0, l], lbs[1, l], hg_norm_w[l], pool_w[l], pool_scale[l],
                    w_branch_a[l], w_branch_b[l], w_out[l])
        h = ffn_half(h, mod_lat, 0, norm_w[l, 0], ffn_wg[l, 0], ffn_wu[l, 0], ffn_wd[l, 0])
        hc = ffn_half(hc, mod_ctx, 0, norm_w[l, 0], ffn_wg[l, 0], ffn_wu[l, 0], ffn_wd[l, 0])
        uc = modulate(rmsnorm(hc, norm_w[l, 1]), mod_ctx[:, 3, None], mod_ctx[:, 4, None])
        y_c, s_f, s_b = mixer(uc, *mix_args, zeros_state, zeros_state, False, not last)
        u = modulate(rmsnorm(h, norm_w[l, 1]), mod_lat[:, 3, None], mod_lat[:, 4, None])
        y, _, _ = mixer(u, *mix_args, s_f, s_b, True, True)
        h = h + mod_lat[:, 5, None] * y
        h = ffn_half(h, mod_lat, 6, norm_w[l, 2], ffn_wg[l, 1], ffn_wu[l, 1], ffn_wd[l, 1])
        if not last:
            hc = hc + mod_ctx[:, 5, None] * y_c
            hc = ffn_half(hc, mod_ctx, 6, norm_w[l, 2], ffn_wg[l, 1], ffn_wu[l, 1], ffn_wd[l, 1])
    return rmsnorm(h, final_norm_w)
```

```python
import functools

import jax
import jax.numpy as jnp
from jax import lax
from jax.experimental import pallas as pl
from jax.experimental.pallas import tpu as pltpu

EPS = 1e-6
N_MOD = 9
HEADS = 4
HEAD_DIM = 128
HG_WIDTH = HEADS * HEAD_DIM
POOL_WINDOWS = (2, 4, 8, 16)
POOL_GD = 128
POOL_WIDTH = POOL_GD * len(POOL_WINDOWS)
GRID_W = 64
CHUNK = 64
SCAN_LEVELS = 6
POOL_PAD = 8

VMEM_LIMIT_BYTES = 56 * 1024 * 1024

F32 = jnp.float32
BF16 = jnp.bfloat16


def _params(*sem):
    return pltpu.CompilerParams(dimension_semantics=sem, vmem_limit_bytes=VMEM_LIMIT_BYTES)


def _const_spec(shape):
    nd = len(shape)
    return pl.BlockSpec(shape, lambda *_: (0,) * nd, pipeline_mode=pl.Buffered(1))


def _sigmoid(x):
    return jax.nn.sigmoid(x)


def _dot(a, b):
    return jnp.dot(a, b, preferred_element_type=F32)


def _dot_nt(a, b):
    return lax.dot_general(a, b, (((1,), (1,)), ((), ())), preferred_element_type=F32)


def _dot_tn(a, b):
    return lax.dot_general(a, b, (((0,), (0,)), ((), ())), preferred_element_type=F32)


def _norm_modulate(x, nw, shift, scale):
    ms = jnp.mean(x * x, axis=-1, keepdims=True)
    y = x * lax.rsqrt(ms + EPS) * nw
    return y * (1.0 + scale) + shift


def _mod_kernel(c_ref, w_ref, b_ref, o_ref):
    c = c_ref[...]
    s = (c * _sigmoid(c)).astype(BF16)
    o_ref[0] = _dot(s, w_ref[0].astype(BF16)) + b_ref[0]


def _mod_call(cvec, ada_w, ada_b):
    depth, d, nd = ada_w.shape
    rows = cvec.shape[0]
    tn = d
    return pl.pallas_call(
        _mod_kernel,
        grid=(depth, nd // tn),
        in_specs=[
            pl.BlockSpec((rows, d), lambda l, j: (0, 0)),
            pl.BlockSpec((1, d, tn), lambda l, j: (l, 0, j)),
            pl.BlockSpec((1, 1, tn), lambda l, j: (l, 0, j)),
        ],
        out_specs=pl.BlockSpec((1, rows, tn), lambda l, j: (l, 0, j)),
        out_shape=jax.ShapeDtypeStruct((depth, rows, nd), F32),
        compiler_params=_params("parallel", "parallel"),
        name="adaln_mod",
    )(cvec, ada_w, ada_b.reshape(depth, 1, nd))


def _ffn_kernel(*refs, k0, final):
    if final:
        h_ref, mod_ref, nw_ref, wg_ref, wu_ref, wd_ref, fnw_ref, o_ref = refs
    else:
        h_ref, mod_ref, nw_ref, wg_ref, wu_ref, wd_ref, o_ref = refs
    x = h_ref[0]
    shift = mod_ref[0, k0:k0 + 1, :]
    scale = mod_ref[0, k0 + 1:k0 + 2, :]
    gate = mod_ref[0, k0 + 2:k0 + 3, :]
    u = _norm_modulate(x, nw_ref[...], shift, scale).astype(BF16)
    g = _dot(u, wg_ref[...])
    up = _dot(u, wu_ref[...])
    a = (g * _sigmoid(g) * up).astype(BF16)
    y = _dot(a, wd_ref[...])
    out = x + (0.5 * gate) * y
    if final:
        ms = jnp.mean(out * out, axis=-1, keepdims=True)
        out = out * lax.rsqrt(ms + EPS) * fnw_ref[...]
    o_ref[0] = out


def _mod_spec(mod):
    nd = mod.shape[-1]
    if mod.shape[0] == 1:
        return pl.BlockSpec((1, N_MOD, nd), lambda b, i: (0, 0, 0))
    return pl.BlockSpec((1, N_MOD, nd), lambda b, i: (b, 0, 0))


def _row_tile(n, target):
    return min(n, target)


def _ffn_call(h, mod, nw, wg, wu, wd, k0, final_nw=None):
    b, n, d = h.shape
    f = wg.shape[1]
    tm = _row_tile(n, 512)
    final = final_nw is not None
    tok = pl.BlockSpec((1, tm, d), lambda bi, i: (bi, i, 0))
    in_specs = [tok, _mod_spec(mod), _const_spec((1, d)), _const_spec((d, f)), _const_spec((d, f)),
                _const_spec((f, d))]
    args = [h, mod, nw.reshape(1, d), wg, wu, wd]
    if final:
        in_specs.append(_const_spec((1, d)))
        args.append(final_nw.reshape(1, d))
    return pl.pallas_call(
        functools.partial(_ffn_kernel, k0=k0, final=final),
        grid=(b, n // tm),
        in_specs=in_specs,
        out_specs=tok,
        out_shape=jax.ShapeDtypeStruct((b, n, d), F32),
        compiler_params=_params("parallel", "parallel"),
        name="ffn_half",
    )(*args)


def _chunk_cumsum(x, reverse):
    rows = x.shape[0]
    r = lax.broadcasted_iota(jnp.int32, x.shape, 0) & (CHUNK - 1)
    s = 1
    while s < CHUNK:
        if reverse:
            x = x + jnp.where(r < CHUNK - s, pltpu.roll(x, rows - s, 0), 0.0)
        else:
            x = x + jnp.where(r >= s, pltpu.roll(x, s, 0), 0.0)
        s *= 2
    return x


def _inproj_kernel(h_ref, mod_ref, nw_ref, w_ref, lbf_ref, lbb_ref,
                   q_o, kf_o, cf_o, kb_o, cb_o, v_o, sg_o, pv_o, ga_o, gb_o):
    x = h_ref[0]
    u = _norm_modulate(x, nw_ref[...], mod_ref[0, 3:4, :], mod_ref[0, 4:5, :]).astype(BF16)
    hw, pw, d = HG_WIDTH, POOL_WIDTH, x.shape[-1]

    def proj(c0, width):
        return _dot(u, w_ref[:, c0:c0 + width])

    q = proj(0, hw)
    q_o[0] = q * _sigmoid(q)

    def gates(z, lb, k_o, c_o, reverse):
        k_o[0] = (1.0 - lb) * _sigmoid(-z)
        lf = jnp.log(lb + (1.0 - lb) * _sigmoid(z))
        c_o[0] = _chunk_cumsum(lf, reverse)

    gates(proj(hw, hw), lbf_ref[...], kf_o, cf_o, False)
    gates(proj(2 * hw, hw), lbb_ref[...], kb_o, cb_o, True)
    v_o[0] = proj(3 * hw, hw).astype(BF16)
    g = proj(4 * hw, hw)
    sg_o[0] = g * _sigmoid(g)
    pv_o[0] = proj(5 * hw, pw)
    c0 = 5 * hw + pw
    ga_o[0] = _sigmoid(proj(c0, d))
    gb_o[0] = _sigmoid(proj(c0 + d, d))


def _inproj_call(h, mod, nw, w_in, lb_f, lb_b):
    b, n, d = h.shape
    cols = w_in.shape[1]
    tm = _row_tile(n, 512)
    hw, pw = HG_WIDTH, POOL_WIDTH

    def tok(width):
        return pl.BlockSpec((1, tm, width), lambda bi, i: (bi, i, 0))

    def out(width, dtype=F32):
        return jax.ShapeDtypeStruct((b, n, width), dtype)

    widths = [hw, hw, hw, hw, hw, hw, hw, pw, d, d]
    dtypes = [F32, F32, F32, F32, F32, BF16, F32, F32, F32, F32]
    return pl.pallas_call(
        _inproj_kernel,
        grid=(b, n // tm),
        in_specs=[tok(d), _mod_spec(mod), _const_spec((1, d)), _const_spec((d, cols)),
                  _const_spec((1, hw)), _const_spec((1, hw))],
        out_specs=[tok(w) for w in widths],
        out_shape=[out(w, t) for w, t in zip(widths, dtypes)],
        compiler_params=_params("parallel", "parallel"),
        name="in_proj",
    )(h, mod, nw.reshape(1, d), w_in, lb_f.reshape(1, hw), lb_b.reshape(1, hw))


def _row_bcast(c, idx):
    return jnp.broadcast_to(c[idx:idx + 1, :], (8, c.shape[1]))


def _level_anchor(c, level, reverse):
    bs = 1 << level
    off = bs if reverse else bs - 1
    groups = []
    r8 = lax.broadcasted_iota(jnp.int32, (8, c.shape[1]), 0)
    for g in range(CHUNK // 8):
        base = g * 8
        if bs >= 4:
            start = (base // (2 * bs)) * (2 * bs)
            groups.append(_row_bcast(c, start + off))
        else:
            m = _row_bcast(c, base + 8 - 2 * bs + off)
            for blk in range(8 // (2 * bs) - 2, -1, -1):
                m = jnp.where(r8 < (blk + 1) * 2 * bs, _row_bcast(c, base + blk * 2 * bs + off), m)
            groups.append(m)
    return jnp.concatenate(groups, axis=0)


def _chunk_attention(q, k, c, reverse):
    rows = lax.broadcasted_iota(jnp.int32, (CHUNK, q.shape[1]), 0)
    ti = lax.broadcasted_iota(jnp.int32, (CHUNK, CHUNK), 0)
    si = lax.broadcasted_iota(jnp.int32, (CHUNK, CHUNK), 1)
    acc = jnp.where(ti == si, _dot_nt(q.astype(BF16), k.astype(BF16)), 0.0)
    for level in range(SCAN_LEVELS):
        bs = 1 << level
        anchor = _level_anchor(c, level, reverse)
        w = jnp.exp(-jnp.abs(c - anchor))
        is_query = ((rows & bs) == 0) if reverse else ((rows & bs) != 0)
        ql = jnp.where(is_query, q * w, 0.0).astype(BF16)
        kl = jnp.where(is_query, 0.0, k * w).astype(BF16)
        p = _dot_nt(ql, kl)
        if level == SCAN_LEVELS - 1:
            acc = acc + p
        else:
            same = (ti >> (level + 1)) == (si >> (level + 1))
            acc = jnp.where(same, acc + p, 0.0)
    return acc


def _scan_chunk(q, k, c, v, st_ref, reverse):
    last = c[0:1, :] if reverse else c[CHUNK - 1:CHUNK, :]
    st = st_ref[...]
    inter = _dot_nt((q * jnp.exp(c)).astype(BF16), st.astype(BF16))
    attn = _chunk_attention(q, k, c, reverse)
    intra = _dot(attn.astype(BF16), v)
    k_dec = (k * jnp.exp(last - c)).astype(BF16)
    st_ref[...] = st * jnp.exp(last) + _dot_tn(v, k_dec)
    return inter + intra


def _scan_kernel(qf_ref, vf_ref, kf_ref, cf_ref, qb_ref, vb_ref, kb_ref, cb_ref, s0_ref,
                 of_ref, ob_ref, sfin_ref, st_ref, *, n_chunks):
    j = pl.program_id(1)

    @pl.when(j == 0)
    def _():
        st_ref[...] = s0_ref[0]

    def body(i, carry):
        off_f = pl.multiple_of(i * CHUNK, CHUNK)
        off_b = pl.multiple_of((n_chunks - 1 - i) * CHUNK, CHUNK)
        for h in range(HEADS):
            lanes = slice(h * HEAD_DIM, (h + 1) * HEAD_DIM)
            rf = pl.ds(off_f, CHUNK)
            of_ref[0, rf, lanes] = _scan_chunk(
                qf_ref[0, rf, lanes], kf_ref[0, rf, lanes], cf_ref[0, rf, lanes], vf_ref[0, rf, lanes],
                st_ref.at[0, h], False)
            rb = pl.ds(off_b, CHUNK)
            ob_ref[0, rb, lanes] = _scan_chunk(
                qb_ref[0, rb, lanes], kb_ref[0, rb, lanes], cb_ref[0, rb, lanes], vb_ref[0, rb, lanes],
                st_ref.at[1, h], True)
        return carry

    lax.fori_loop(0, n_chunks, body, 0)

    @pl.when(j == pl.num_programs(1) - 1)
    def _():
        sfin_ref[0] = st_ref[...]


def _scan_call(q, kf, cf, kb, cb, v, s0):
    b, n, hw = q.shape
    ts = _row_tile(n, 256)
    nb = n // ts
    fwd = pl.BlockSpec((1, ts, hw), lambda bi, j: (bi, j, 0))
    bwd = pl.BlockSpec((1, ts, hw), lambda bi, j: (bi, nb - 1 - j, 0))
    st_shape = (2, HEADS, HEAD_DIM, HEAD_DIM)
    st_spec = pl.BlockSpec((1,) + st_shape, lambda bi, j: (bi, 0, 0, 0, 0))
    return pl.pallas_call(
        functools.partial(_scan_kernel, n_chunks=ts // CHUNK),
        grid=(b, nb),
        in_specs=[fwd, fwd, fwd, fwd, bwd, bwd, bwd, bwd, st_spec],
        out_specs=[fwd, bwd, st_spec],
        out_shape=[jax.ShapeDtypeStruct((b, n, hw), F32), jax.ShapeDtypeStruct((b, n, hw), F32),
                   jax.ShapeDtypeStruct((b,) + st_shape, F32)],
        scratch_shapes=[pltpu.VMEM(st_shape, F32)],
        compiler_params=_params("parallel", "arbitrary"),
        name="gla_scan",
    )(q, v, kf, cf, q, v, kb, cb, s0)


def _inv_count(shape, axis, length, w):
    i = lax.broadcasted_iota(jnp.int32, shape, axis) - POOL_PAD
    hi = jnp.minimum(i + (w - w // 2), length)
    lo = jnp.maximum(i - w // 2, 0)
    return 1.0 / jnp.maximum(hi - lo, 1).astype(F32)


def _pool_body(x_ref, pw_ref, sc_ref, o_ref, pad_ref, w, two_d):
    rows, width, gd = x_ref.shape[1:]
    wp = width + 2 * POOL_PAD
    rp = POOL_PAD if two_d else 0
    x = x_ref[0]
    pad_ref[...] = jnp.zeros(pad_ref.shape, F32)
    pad_ref[rp:rp + rows, POOL_PAD:POOL_PAD + width, :] = x
    t = pad_ref[...]
    t = t + pltpu.roll(t, 1, 1)
    k = 2
    while k < w:
        s = k // 2
        t = pltpu.roll(t, s, 1) + pltpu.roll(t, wp - s, 1)
        k *= 2
    t = t * _inv_count((1, wp, gd), 1, width, w)
    if two_d:
        n = t.shape[0]
        t = t[0:n - 1] + t[1:n]
        first = 1
        k = 2
        while k < w:
            n = t.shape[0]
            t = t[0:n - k] + t[k:n]
            first += k // 2
            k *= 2
        t = t[POOL_PAD - first:POOL_PAD - first + rows]
        inv_r = _inv_count((rows + 2 * POOL_PAD, 1, gd), 0, rows, w)[POOL_PAD:POOL_PAD + rows]
        t = t * inv_r
    m = t[:, POOL_PAD:POOL_PAD + width, :]
    dlt = (m - x).reshape(rows * width, gd).astype(BF16)
    y = _dot(dlt, pw_ref[0])
    o_ref[0] = (y * sc_ref[...]).astype(o_ref.dtype)


def _pool_kernel(x_ref, pw_ref, sc_ref, o_ref, pad_ref, *, two_d):
    g = pl.program_id(0)
    for gi, w in enumerate(POOL_WINDOWS):
        @pl.when(g == gi)
        def _(w=w):
            _pool_body(x_ref, pw_ref, sc_ref, o_ref, pad_ref, w, two_d)


def _pool_call(pv, pool_w, pool_scale, two_d):
    b, n, pw = pv.shape
    width = GRID_W if two_d else n
    rows = n // width
    groups = len(POOL_WINDOWS)
    pad_rows = rows + 2 * POOL_PAD if two_d else rows
    return pl.pallas_call(
        functools.partial(_pool_kernel, two_d=two_d),
        grid=(groups, b),
        in_specs=[
            pl.BlockSpec((1, rows, width, POOL_GD), lambda g, bi: (bi, 0, 0, g)),
            pl.BlockSpec((1, POOL_GD, POOL_GD), lambda g, bi: (g, 0, 0)),
            pl.BlockSpec((1, POOL_GD), lambda g, bi: (0, g)),
        ],
        out_specs=pl.BlockSpec((1, n, POOL_GD), lambda g, bi: (bi, 0, g)),
        out_shape=jax.ShapeDtypeStruct((b, n, pw), BF16),
        scratch_shapes=[pltpu.VMEM((pad_rows, width + 2 * POOL_PAD, POOL_GD), F32)],
        compiler_params=_params("parallel", "parallel"),
        name="pool_mixer",
    )(pv.reshape(b, rows, width, pw), pool_w, pool_scale.reshape(1, pw))


def _merge_kernel(of_ref, ob_ref, sg_ref, p_ref, ga_ref, gb_ref, h_ref, mod_ref, hnw_ref,
                  wa_ref, wb_ref, wo_ref, o_ref):
    o = of_ref[0] + ob_ref[0]
    parts = []
    for h in range(HEADS):
        oh = o[:, h * HEAD_DIM:(h + 1) * HEAD_DIM]
        ms = jnp.mean(oh * oh, axis=-1, keepdims=True)
        parts.append(oh * lax.rsqrt(ms + EPS))
    on = jnp.concatenate(parts, axis=-1) * hnw_ref[...]
    a = (on * sg_ref[0]).astype(BF16)
    merged = ga_ref[0] * _dot(a, wa_ref[...]) + gb_ref[0] * _dot(p_ref[0], wb_ref[...])
    y = _dot(merged.astype(BF16), wo_ref[...])
    o_ref[0] = h_ref[0] + mod_ref[0, 5:6, :] * y


def _merge_call(o_f, o_b, sg, p, ga, gb, h, mod, hnw, w_a, w_b, w_out):
    b, n, d = h.shape
    hw, pw = HG_WIDTH, POOL_WIDTH
    tm = _row_tile(n, 512)

    def tok(width):
        return pl.BlockSpec((1, tm, width), lambda bi, i: (bi, i, 0))

    return pl.pallas_call(
        _merge_kernel,
        grid=(b, n // tm),
        in_specs=[tok(hw), tok(hw), tok(hw), tok(pw), tok(d), tok(d), tok(d), _mod_spec(mod),
                  _const_spec((1, hw)), _const_spec((hw, d)), _const_spec((pw, d)), _const_spec((d, d))],
        out_specs=tok(d),
        out_shape=jax.ShapeDtypeStruct((b, n, d), F32),
        compiler_params=_params("parallel", "parallel"),
        name="merge_out",
    )(o_f, o_b, sg, p, ga, gb, h, mod, hnw.reshape(1, hw), w_a, w_b, w_out)


def kernel(x, c, ctx, c_ctx, ada_w, ada_b, norm_w, ffn_wg, ffn_wu, ffn_wd, w_in, lower_bounds, hg_norm_w,
           pool_w, pool_scale, w_branch_a, w_branch_b, w_out, final_norm_w):
    b, n, d = x.shape
    depth = ada_w.shape[0]
    assert n % GRID_W == 0 and n % CHUNK == 0 and ctx.shape[1] % CHUNK == 0

    n_rows = -(-(b + 1) // 8) * 8
    cvec = jnp.zeros((n_rows, d), F32).at[:b].set(c).at[b].set(c_ctx)
    mods = _mod_call(cvec, ada_w, ada_b).reshape(depth, n_rows, N_MOD, d)

    sm = jax.nn.softmax(lower_bounds.astype(F32), axis=1)
    lbs = jnp.cumsum(sm, axis=1) - sm[:, :1]

    wg, wu, wd = ffn_wg.astype(BF16), ffn_wu.astype(BF16), ffn_wd.astype(BF16)
    w_in_b = w_in.astype(BF16)
    w_a, w_b, w_o = w_branch_a.astype(BF16), w_branch_b.astype(BF16), w_out.astype(BF16)
    pool_w_b = pool_w.astype(BF16)

    zeros_state = jnp.zeros((b, 2, HEADS, HEAD_DIM, HEAD_DIM), F32)
    h, hc = x, ctx
    for l in range(depth):
        last = l == depth - 1
        mod_lat, mod_ctx = mods[l, :b], mods[l, b:b + 1]
        ffn1 = (norm_w[l, 0], wg[l, 0], wu[l, 0], wd[l, 0], 0)
        ffn2 = (norm_w[l, 2], wg[l, 1], wu[l, 1], wd[l, 1], 6)

        h = _ffn_call(h, mod_lat, *ffn1)
        hc = _ffn_call(hc, mod_ctx, *ffn1)

        qc, kfc, cfc, kbc, cbc, vc, sgc, pvc, gac, gbc = _inproj_call(
            hc, mod_ctx, norm_w[l, 1], w_in_b[l], lbs[0, l], lbs[1, l])
        ofc, obc, s_ctx = _scan_call(qc, kfc, cfc, kbc, cbc, vc, zeros_state)

        q, kf, cf, kb, cb, v, sg, pv, ga, gb = _inproj_call(
            h, mod_lat, norm_w[l, 1], w_in_b[l], lbs[0, l], lbs[1, l])
        o_f, o_b, _ = _scan_call(q, kf, cf, kb, cb, v, s_ctx)
        p = _pool_call(pv, pool_w_b[l], pool_scale[l], True)
        h = _merge_call(o_f, o_b, sg, p, ga, gb, h, mod_lat, hg_norm_w[l], w_a[l], w_b[l], w_o[l])
        h = _ffn_call(h, mod_lat, *ffn2, final_nw=final_norm_w if last else None)

        if not last:
            pc = _pool_call(pvc, pool_w_b[l], pool_scale[l], False)
            hc = _merge_call(ofc, obc, sgc, pc, gac, gbc, hc, mod_ctx, hg_norm_w[l], w_a[l], w_b[l], w_o[l])
            hc = _ffn_call(hc, mod_ctx, *ffn2)
    return h
```

```python
import functools

import numpy as np
import jax
import jax.numpy as jnp
from jax import lax
from jax.experimental import pallas as pl
from jax.experimental.pallas import tpu as pltpu

EPS = 1e-6
N_MOD = 9
HEADS = 4
HEAD_DIM = 128
HG_WIDTH = HEADS * HEAD_DIM
POOL_WINDOWS = (2, 4, 8, 16)
POOL_GD = 128
POOL_WIDTH = POOL_GD * len(POOL_WINDOWS)
GRID_W = 64
CHUNK = 64
SUBLANES = 8
GROUPS = CHUNK // SUBLANES
SCAN_LEVELS = 6
PAIR = 2 * HEAD_DIM
POOL_PAD = 8
LOG2_E = 1.4426950408889634

VMEM_LIMIT_BYTES = 56 * 1024 * 1024

F32 = jnp.float32
BF16 = jnp.bfloat16


def _params(*sem):
    return pltpu.CompilerParams(dimension_semantics=sem, vmem_limit_bytes=VMEM_LIMIT_BYTES)


def _const_spec(shape):
    nd = len(shape)
    return pl.BlockSpec(shape, lambda *_: (0,) * nd, pipeline_mode=pl.Buffered(1))


def _sigmoid(x):
    return jax.nn.sigmoid(x)


def _dot(a, b):
    return jnp.dot(a, b, preferred_element_type=F32)


def _dot_nt(a, b):
    return lax.dot_general(a, b, (((1,), (1,)), ((), ())), preferred_element_type=F32)


def _dot_tn(a, b):
    return lax.dot_general(a, b, (((0,), (0,)), ((), ())), preferred_element_type=F32)


def _norm_modulate(x, nw, shift, scale):
    ms = jnp.mean(x * x, axis=-1, keepdims=True)
    y = x * lax.rsqrt(ms + EPS) * nw
    return y * (1.0 + scale) + shift


def _mod_kernel(c_ref, w_ref, b_ref, o_ref):
    c = c_ref[...]
    s = (c * _sigmoid(c)).astype(BF16)
    o_ref[0] = _dot(s, w_ref[0].astype(BF16)) + b_ref[0]


def _mod_call(cvec, ada_w, ada_b):
    depth, d, nd = ada_w.shape
    rows = cvec.shape[0]
    tn = d
    return pl.pallas_call(
        _mod_kernel,
        grid=(depth, nd // tn),
        in_specs=[
            pl.BlockSpec((rows, d), lambda l, j: (0, 0)),
            pl.BlockSpec((1, d, tn), lambda l, j: (l, 0, j)),
            pl.BlockSpec((1, 1, tn), lambda l, j: (l, 0, j)),
        ],
        out_specs=pl.BlockSpec((1, rows, tn), lambda l, j: (l, 0, j)),
        out_shape=jax.ShapeDtypeStruct((depth, rows, nd), F32),
        compiler_params=_params("parallel", "parallel"),
        name="adaln_mod",
    )(cvec, ada_w, ada_b.reshape(depth, 1, nd))


def _ffn_kernel(*refs, k0, final):
    if final:
        h_ref, mod_ref, nw_ref, wg_ref, wu_ref, wd_ref, fnw_ref, o_ref = refs
    else:
        h_ref, mod_ref, nw_ref, wg_ref, wu_ref, wd_ref, o_ref = refs
    x = h_ref[0]
    shift = mod_ref[0, k0:k0 + 1, :]
    scale = mod_ref[0, k0 + 1:k0 + 2, :]
    gate = mod_ref[0, k0 + 2:k0 + 3, :]
    u = _norm_modulate(x, nw_ref[...], shift, scale).astype(BF16)
    g = _dot(u, wg_ref[...])
    up = _dot(u, wu_ref[...])
    a = (g * _sigmoid(g) * up).astype(BF16)
    y = _dot(a, wd_ref[...])
    out = x + (0.5 * gate) * y
    if final:
        ms = jnp.mean(out * out, axis=-1, keepdims=True)
        out = out * lax.rsqrt(ms + EPS) * fnw_ref[...]
    o_ref[0] = out


def _mod_spec(mod):
    nd = mod.shape[-1]
    if mod.shape[0] == 1:
        return pl.BlockSpec((1, N_MOD, nd), lambda b, i: (0, 0, 0))
    return pl.BlockSpec((1, N_MOD, nd), lambda b, i: (b, 0, 0))


def _row_tile(n, target):
    return min(n, target)


def _ffn_call(h, mod, nw, wg, wu, wd, k0, final_nw=None):
    b, n, d = h.shape
    f = wg.shape[1]
    tm = _row_tile(n, 512)
    final = final_nw is not None
    tok = pl.BlockSpec((1, tm, d), lambda bi, i: (bi, i, 0))
    in_specs = [tok, _mod_spec(mod), _const_spec((1, d)), _const_spec((d, f)), _const_spec((d, f)),
                _const_spec((f, d))]
    args = [h, mod, nw.reshape(1, d), wg, wu, wd]
    if final:
        in_specs.append(_const_spec((1, d)))
        args.append(final_nw.reshape(1, d))
    return pl.pallas_call(
        functools.partial(_ffn_kernel, k0=k0, final=final),
        grid=(b, n // tm),
        in_specs=in_specs,
        out_specs=tok,
        out_shape=jax.ShapeDtypeStruct((b, n, d), F32),
        compiler_params=_params("parallel", "parallel"),
        name="ffn_half",
    )(*args)


def _chunk_cumsum(x, reverse):
    n = x.shape[-1]
    x3 = x.reshape(GROUPS, SUBLANES, n)
    r8 = lax.broadcasted_iota(jnp.int32, (1, SUBLANES, n), 1)
    s = 1
    while s < SUBLANES:
        if reverse:
            x3 = x3 + jnp.where(r8 < SUBLANES - s, pltpu.roll(x3, SUBLANES - s, 1), 0.0)
        else:
            x3 = x3 + jnp.where(r8 >= s, pltpu.roll(x3, s, 1), 0.0)
        s *= 2
    edge = 0 if reverse else SUBLANES - 1
    totals = [x3[g, edge:edge + 1, :] for g in range(GROUPS)]
    order = range(GROUPS - 1, -1, -1) if reverse else range(GROUPS)
    out = [None] * GROUPS
    run = None
    for g in order:
        out[g] = x3[g] if run is None else x3[g] + run
        run = totals[g] if run is None else run + totals[g]
    return jnp.concatenate(out, axis=0)


def _inproj_kernel(h_ref, mod_ref, nw_ref, w_ref, lbf_ref, lbb_ref,
                   q_o, kf_o, cf_o, kb_o, cb_o, v_o, pv_o):
    x = h_ref[0]
    tm = x.shape[0]
    u = _norm_modulate(x, nw_ref[...], mod_ref[0, 3:4, :], mod_ref[0, 4:5, :]).astype(BF16)
    hw, pw = HG_WIDTH, POOL_WIDTH

    z_all = _dot(u, w_ref[...])

    def proj(c0, width):
        return z_all[:, c0:c0 + width]

    q = proj(0, hw)
    q_o[0] = q * _sigmoid(q)

    def gates(z, lb, k_o, c_o, reverse):
        f = lb + (1.0 - lb) * _sigmoid(z)
        k_o[0] = 1.0 - f
        lf = jnp.log(f) * LOG2_E
        for ch in range(tm // CHUNK):
            rows = slice(ch * CHUNK, (ch + 1) * CHUNK)
            c_o[0, rows, :] = _chunk_cumsum(lf[rows], reverse)

    gates(proj(hw, hw), lbf_ref[...], kf_o, cf_o, False)
    gates(proj(2 * hw, hw), lbb_ref[...], kb_o, cb_o, True)
    v_o[0] = proj(3 * hw, hw).astype(BF16)
    pv_o[0] = proj(4 * hw, pw)


def _inproj_call(h, mod, nw, w_main, lb_f, lb_b):
    b, n, d = h.shape
    cols = w_main.shape[1]
    tm = _row_tile(n, 512)
    hw, pw = HG_WIDTH, POOL_WIDTH

    def tok(width):
        return pl.BlockSpec((1, tm, width), lambda bi, i: (bi, i, 0))

    def out(width, dtype=F32):
        return jax.ShapeDtypeStruct((b, n, width), dtype)

    widths = [hw, hw, hw, hw, hw, hw, pw]
    dtypes = [F32, F32, F32, F32, F32, BF16, F32]
    return pl.pallas_call(
        _inproj_kernel,
        grid=(b, n // tm),
        in_specs=[tok(d), _mod_spec(mod), _const_spec((1, d)), _const_spec((d, cols)),
                  _const_spec((1, hw)), _const_spec((1, hw))],
        out_specs=[tok(w) for w in widths],
        out_shape=[out(w, t) for w, t in zip(widths, dtypes)],
        compiler_params=_params("parallel", "parallel"),
        name="in_proj",
    )(h, mod, nw.reshape(1, d), w_main, lb_f.reshape(1, hw), lb_b.reshape(1, hw))


def _level_masks():
    t = np.arange(CHUNK)[:, None]
    s = np.arange(CHUNK)[None, :]
    out = np.zeros((2, SCAN_LEVELS + 1, CHUNK, CHUNK), np.float32)
    for rev in (0, 1):
        for level in range(SCAN_LEVELS):
            bs = 1 << level
            same = (t >> (level + 1)) == (s >> (level + 1))
            tq = ((t & bs) == 0) if rev else ((t & bs) != 0)
            sk = ((s & bs) != 0) if rev else ((s & bs) == 0)
            out[rev, level] = same & tq & sk
        out[rev, SCAN_LEVELS] = t == s
    return np.concatenate([out, out], axis=-1)


def _neg_abs(x):
    bits = lax.bitcast_convert_type(x, jnp.uint32) | jnp.uint32(0x80000000)
    return lax.bitcast_convert_type(bits, F32)


def _level_operand(q3, k3, c3, level, reverse):
    width = q3.shape[-1]
    bs = 1 << level
    r8 = lax.broadcasted_iota(jnp.int32, (1, SUBLANES, width), 1)
    if bs < SUBLANES:
        is_q = ((r8 & bs) == 0) if reverse else ((r8 & bs) != 0)
        sel = jnp.where(is_q, q3, k3)
        if bs == 1:
            nb = pltpu.roll(c3, 7 if reverse else 1, 1)
            expo = jnp.where(is_q, c3 - nb, 0.0)
        else:
            off = bs if reverse else bs - 1
            anchor = c3[:, SUBLANES - 2 * bs + off:SUBLANES - 2 * bs + off + 1, :]
            for blk in range(SUBLANES // (2 * bs) - 2, -1, -1):
                row = blk * 2 * bs + off
                anchor = jnp.where(r8 < (blk + 1) * 2 * bs, c3[:, row:row + 1, :], anchor)
            expo = _neg_abs(c3 - anchor)
        return sel * jnp.exp2(expo)
    gbs = bs // SUBLANES
    sel, anchor = [], []
    for g in range(GROUPS):
        start = (g // (2 * gbs)) * (2 * gbs)
        is_q = ((g & gbs) == 0) if reverse else ((g & gbs) != 0)
        sel.append((q3 if is_q else k3)[g:g + 1])
        if reverse:
            anchor.append(c3[start + gbs:start + gbs + 1, 0:1, :])
        else:
            anchor.append(c3[start + gbs - 1:start + gbs, SUBLANES - 1:SUBLANES, :])
    sel = jnp.concatenate(sel, axis=0)
    anchor = jnp.concatenate(anchor, axis=0)
    return sel * jnp.exp2(_neg_abs(c3 - anchor))


def _block_diag(a, b):
    z = jnp.zeros_like(a)
    return jnp.concatenate([jnp.concatenate([a, z], axis=1), jnp.concatenate([z, b], axis=1)], axis=0)


def _scan_direction(q, k, c, v, mask_ref, st_ref, reverse):
    width = q.shape[-1]
    q3, k3, c3 = (a.reshape(GROUPS, SUBLANES, width) for a in (q, k, c))
    pairs = [(slice(lo, lo + HEAD_DIM), slice(lo + HEAD_DIM, lo + PAIR), slice(lo, lo + PAIR))
             for lo in range(0, width, PAIR)]

    def scores(lhs, rhs, level):
        return [_dot_nt(lhs[:, pr], _block_diag(rhs[:, ha], rhs[:, hb])) * mask_ref[level]
                for ha, hb, pr in pairs]

    accs = scores(q.astype(BF16), k.astype(BF16), SCAN_LEVELS)
    for level in range(SCAN_LEVELS):
        x = _level_operand(q3, k3, c3, level, reverse).reshape(CHUNK, width).astype(BF16)
        accs = [a + s for a, s in zip(accs, scores(x, x, level))]

    last = c[0:1, :] if reverse else c[CHUNK - 1:CHUNK, :]
    qe = (q * jnp.exp2(c)).astype(BF16)
    k_dec = (k * jnp.exp2(last - c)).astype(BF16)
    decay = jnp.exp2(last)
    outs = []
    for p, (ha, hb, _) in enumerate(pairs):
        intra = _dot(accs[p].astype(BF16), _block_diag(v[:, ha], v[:, hb]))
        inter = []
        for h, lanes in ((2 * p, ha), (2 * p + 1, hb)):
            st = st_ref[h]
            inter.append(_dot_nt(qe[:, lanes], st.astype(BF16)))
            st_ref[h] = st * decay[:, lanes] + _dot_tn(v[:, lanes], k_dec[:, lanes])
        outs.append(intra + jnp.concatenate(inter, axis=1))
    return jnp.concatenate(outs, axis=1)


def _scan_kernel(qf_ref, vf_ref, kf_ref, cf_ref, qb_ref, vb_ref, kb_ref, cb_ref, s0_ref, mask_ref,
                 of_ref, ob_ref, sfin_ref, st_ref, *, n_chunks):
    j = pl.program_id(1)

    @pl.when(j == 0)
    def _():
        st_ref[...] = s0_ref[0]

    def body(i, carry):
        rf = pl.ds(pl.multiple_of(i * CHUNK, CHUNK), CHUNK)
        of_ref[0, rf, :] = _scan_direction(qf_ref[0, rf, :], kf_ref[0, rf, :], cf_ref[0, rf, :],
                                           vf_ref[0, rf, :], mask_ref.at[0], st_ref.at[0], False)
        rb = pl.ds(pl.multiple_of((n_chunks - 1 - i) * CHUNK, CHUNK), CHUNK)
        ob_ref[0, rb, :] = _scan_direction(qb_ref[0, rb, :], kb_ref[0, rb, :], cb_ref[0, rb, :],
                                           vb_ref[0, rb, :], mask_ref.at[1], st_ref.at[1], True)
        return carry

    lax.fori_loop(0, n_chunks, body, 0)

    @pl.when(j == pl.num_programs(1) - 1)
    def _():
        sfin_ref[0] = st_ref[...]


def _scan_call(q, kf, cf, kb, cb, v, s0):
    b, n, hw = q.shape
    ts = _row_tile(n, 512)
    nb = n // ts
    fwd = pl.BlockSpec((1, ts, hw), lambda bi, j: (bi, j, 0))
    bwd = pl.BlockSpec((1, ts, hw), lambda bi, j: (bi, nb - 1 - j, 0))
    st_shape = (2, HEADS, HEAD_DIM, HEAD_DIM)
    st_spec = pl.BlockSpec((1,) + st_shape, lambda bi, j: (bi, 0, 0, 0, 0))
    masks = jnp.asarray(_level_masks())
    return pl.pallas_call(
        functools.partial(_scan_kernel, n_chunks=ts // CHUNK),
        grid=(b, nb),
        in_specs=[fwd, fwd, fwd, fwd, bwd, bwd, bwd, bwd, st_spec, _const_spec(masks.shape)],
        out_specs=[fwd, bwd, st_spec],
        out_shape=[jax.ShapeDtypeStruct((b, n, hw), F32), jax.ShapeDtypeStruct((b, n, hw), F32),
                   jax.ShapeDtypeStruct((b,) + st_shape, F32)],
        scratch_shapes=[pltpu.VMEM(st_shape, F32)],
        compiler_params=_params("parallel", "arbitrary"),
        name="gla_scan",
    )(q, v, kf, cf, q, v, kb, cb, s0, masks)


def _inv_count(shape, axis, length, w):
    i = lax.broadcasted_iota(jnp.int32, shape, axis) - POOL_PAD
    hi = jnp.minimum(i + (w - w // 2), length)
    lo = jnp.maximum(i - w // 2, 0)
    return 1.0 / jnp.maximum(hi - lo, 1).astype(F32)


def _pool_body(x_ref, pw_ref, sc_ref, o_ref, pad_ref, w, two_d):
    rows, width, gd = x_ref.shape[1:]
    wp = width + 2 * POOL_PAD
    rp = POOL_PAD if two_d else 0
    x = x_ref[0]
    pad_ref[...] = jnp.zeros(pad_ref.shape, F32)
    pad_ref[rp:rp + rows, POOL_PAD:POOL_PAD + width, :] = x
    t = pad_ref[...]
    t = t + pltpu.roll(t, 1, 1)
    k = 2
    while k < w:
        s = k // 2
        t = pltpu.roll(t, s, 1) + pltpu.roll(t, wp - s, 1)
        k *= 2
    t = t * _inv_count((1, wp, gd), 1, width, w)
    if two_d:
        n = t.shape[0]
        t = t[0:n - 1] + t[1:n]
        first = 1
        k = 2
        while k < w:
            n = t.shape[0]
            t = t[0:n - k] + t[k:n]
            first += k // 2
            k *= 2
        t = t[POOL_PAD - first:POOL_PAD - first + rows]
        inv_r = _inv_count((rows + 2 * POOL_PAD, 1, gd), 0, rows, w)[POOL_PAD:POOL_PAD + rows]
        t = t * inv_r
    m = t[:, POOL_PAD:POOL_PAD + width, :]
    dlt = (m - x).reshape(rows * width, gd).astype(BF16)
    y = _dot(dlt, pw_ref[0])
    o_ref[0] = (y * sc_ref[...]).astype(o_ref.dtype)


def _pool_kernel(x_ref, pw_ref, sc_ref, o_ref, pad_ref, *, two_d):
    g = pl.program_id(0)
    for gi, w in enumerate(POOL_WINDOWS):
        @pl.when(g == gi)
        def _(w=w):
            _pool_body(x_ref, pw_ref, sc_ref, o_ref, pad_ref, w, two_d)


def _pool_call(pv, pool_w, pool_scale, two_d):
    b, n, pw = pv.shape
    width = GRID_W if two_d else n
    rows = n // width
    groups = len(POOL_WINDOWS)
    pad_rows = rows + 2 * POOL_PAD if two_d else rows
    return pl.pallas_call(
        functools.partial(_pool_kernel, two_d=two_d),
        grid=(groups, b),
        in_specs=[
            pl.BlockSpec((1, rows, width, POOL_GD), lambda g, bi: (bi, 0, 0, g)),
            pl.BlockSpec((1, POOL_GD, POOL_GD), lambda g, bi: (g, 0, 0)),
            pl.BlockSpec((1, POOL_GD), lambda g, bi: (0, g)),
        ],
        out_specs=pl.BlockSpec((1, n, POOL_GD), lambda g, bi: (bi, 0, g)),
        out_shape=jax.ShapeDtypeStruct((b, n, pw), BF16),
        scratch_shapes=[pltpu.VMEM((pad_rows, width + 2 * POOL_PAD, POOL_GD), F32)],
        compiler_params=_params("parallel", "parallel"),
        name="pool_mixer",
    )(pv.reshape(b, rows, width, pw), pool_w, pool_scale.reshape(1, pw))


def _merge_kernel(of_ref, ob_ref, p_ref, h_ref, mod_ref, nw_ref, hnw_ref,
                  wgate_ref, wa_ref, wb_ref, wo_ref, o_ref):
    x = h_ref[0]
    d = x.shape[-1]
    hw = HG_WIDTH
    u = _norm_modulate(x, nw_ref[...], mod_ref[0, 3:4, :], mod_ref[0, 4:5, :]).astype(BF16)
    o = of_ref[0] + ob_ref[0]
    parts = []
    for h in range(HEADS):
        oh = o[:, h * HEAD_DIM:(h + 1) * HEAD_DIM]
        ms = jnp.mean(oh * oh, axis=-1, keepdims=True)
        parts.append(oh * lax.rsqrt(ms + EPS))
    on = jnp.concatenate(parts, axis=-1) * hnw_ref[...]
    g = _dot(u, wgate_ref[:, 0:hw])
    a = (on * (g * _sigmoid(g))).astype(BF16)
    ga = _sigmoid(_dot(u, wgate_ref[:, hw:hw + d]))
    gb = _sigmoid(_dot(u, wgate_ref[:, hw + d:hw + 2 * d]))
    merged = ga * _dot(a, wa_ref[...]) + gb * _dot(p_ref[0], wb_ref[...])
    y = _dot(merged.astype(BF16), wo_ref[...])
    o_ref[0] = x + mod_ref[0, 5:6, :] * y


def _merge_call(o_f, o_b, p, h, mod, nw, hnw, w_gate, w_a, w_b, w_out):
    b, n, d = h.shape
    hw, pw = HG_WIDTH, POOL_WIDTH
    tm = _row_tile(n, 512)

    def tok(width):
        return pl.BlockSpec((1, tm, width), lambda bi, i: (bi, i, 0))

    return pl.pallas_call(
        _merge_kernel,
        grid=(b, n // tm),
        in_specs=[tok(hw), tok(hw), tok(pw), tok(d), _mod_spec(mod), _const_spec((1, d)),
                  _const_spec((1, hw)), _const_spec(w_gate.shape), _const_spec((hw, d)),
                  _const_spec((pw, d)), _const_spec((d, d))],
        out_specs=tok(d),
        out_shape=jax.ShapeDtypeStruct((b, n, d), F32),
        compiler_params=_params("parallel", "parallel"),
        name="merge_out",
    )(o_f, o_b, p, h, mod, nw.reshape(1, d), hnw.reshape(1, hw), w_gate, w_a, w_b, w_out)


def kernel(x, c, ctx, c_ctx, ada_w, ada_b, norm_w, ffn_wg, ffn_wu, ffn_wd, w_in, lower_bounds, hg_norm_w,
           pool_w, pool_scale, w_branch_a, w_branch_b, w_out, final_norm_w):
    b, n, d = x.shape
    depth = ada_w.shape[0]
    hw, pw = HG_WIDTH, POOL_WIDTH
    assert n % GRID_W == 0 and n % CHUNK == 0 and ctx.shape[1] % CHUNK == 0

    n_rows = -(-(b + 1) // 8) * 8
    cvec = jnp.zeros((n_rows, d), F32).at[:b].set(c).at[b].set(c_ctx)
    mods = _mod_call(cvec, ada_w, ada_b).reshape(depth, n_rows, N_MOD, d)

    sm = jax.nn.softmax(lower_bounds.astype(F32), axis=1)
    lbs = jnp.cumsum(sm, axis=1) - sm[:, :1]

    wg, wu, wd = ffn_wg.astype(BF16), ffn_wu.astype(BF16), ffn_wd.astype(BF16)
    w_in_b = w_in.astype(BF16)
    w_main = jnp.concatenate([w_in_b[:, :, :4 * hw], w_in_b[:, :, 5 * hw:5 * hw + pw]], axis=-1)
    w_gate = jnp.concatenate([w_in_b[:, :, 4 * hw:5 * hw], w_in_b[:, :, 5 * hw + pw:]], axis=-1)
    w_a, w_b, w_o = w_branch_a.astype(BF16), w_branch_b.astype(BF16), w_out.astype(BF16)
    pool_w_b = pool_w.astype(BF16)

    zeros_state = jnp.zeros((b, 2, HEADS, HEAD_DIM, HEAD_DIM), F32)
    h, hc = x, ctx
    for l in range(depth):
        last = l == depth - 1
        mod_lat, mod_ctx = mods[l, :b], mods[l, b:b + 1]
        ffn1 = (norm_w[l, 0], wg[l, 0], wu[l, 0], wd[l, 0], 0)
        ffn2 = (norm_w[l, 2], wg[l, 1], wu[l, 1], wd[l, 1], 6)
        mix = (norm_w[l, 1], hg_norm_w[l], w_gate[l], w_a[l], w_b[l], w_o[l])

        h = _ffn_call(h, mod_lat, *ffn1)
        hc = _ffn_call(hc, mod_ctx, *ffn1)

        qc, kfc, cfc, kbc, cbc, vc, pvc = _inproj_call(hc, mod_ctx, norm_w[l, 1], w_main[l], lbs[0, l], lbs[1, l])
        ofc, obc, s_ctx = _scan_call(qc, kfc, cfc, kbc, cbc, vc, zeros_state)

        q, kf, cf, kb, cb, v, pv = _inproj_call(h, mod_lat, norm_w[l, 1], w_main[l], lbs[0, l], lbs[1, l])
        o_f, o_b, _ = _scan_call(q, kf, cf, kb, cb, v, s_ctx)
        p = _pool_call(pv, pool_w_b[l], pool_scale[l], True)
        h = _merge_call(o_f, o_b, p, h, mod_lat, *mix)
        h = _ffn_call(h, mod_lat, *ffn2, final_nw=final_norm_w if last else None)

        if not last:
            pc = _pool_call(pvc, pool_w_b[l], pool_scale[l], False)
            hc = _merge_call(ofc, obc, pc, hc, mod_ctx, *mix)
            hc = _ffn_call(hc, mod_ctx, *ffn2)
    return h
```

```python
import functools

import numpy as np
import jax
import jax.numpy as jnp
from jax import lax
from jax.experimental import pallas as pl
from jax.experimental.pallas import tpu as pltpu

EPS = 1e-6
N_MOD = 9
HEADS = 4
HEAD_DIM = 128
HG_WIDTH = HEADS * HEAD_DIM
POOL_WINDOWS = (2, 4, 8, 16)
POOL_GD = 128
POOL_WIDTH = POOL_GD * len(POOL_WINDOWS)
GRID_W = 64
CHUNK = 64
SUBLANES = 8
GROUPS = CHUNK // SUBLANES
SCAN_LEVELS = 6
PAIR = 2 * HEAD_DIM
POOL_PAD = 8
SUBTILE = 256
LOG2_E = 1.4426950408889634

VMEM_LIMIT_BYTES = 56 * 1024 * 1024

F32 = jnp.float32
BF16 = jnp.bfloat16


def _params(*sem):
    return pltpu.CompilerParams(dimension_semantics=sem, vmem_limit_bytes=VMEM_LIMIT_BYTES)


def _const_spec(shape):
    nd = len(shape)
    return pl.BlockSpec(shape, lambda *_: (0,) * nd, pipeline_mode=pl.Buffered(1))


def _sigmoid(x):
    return jax.nn.sigmoid(x)


def _dot(a, b):
    return jnp.dot(a, b, preferred_element_type=F32)


def _dot_nt(a, b):
    return lax.dot_general(a, b, (((1,), (1,)), ((), ())), preferred_element_type=F32)


def _dot_tn(a, b):
    return lax.dot_general(a, b, (((0,), (0,)), ((), ())), preferred_element_type=F32)


def _norm_modulate(x, nw, shift, scale):
    ms = jnp.mean(x * x, axis=-1, keepdims=True)
    y = x * lax.rsqrt(ms + EPS) * nw
    return y * (1.0 + scale) + shift


def _mod_kernel(c_ref, w_ref, b_ref, o_ref):
    c = c_ref[...]
    s = (c * _sigmoid(c)).astype(BF16)
    o_ref[0] = _dot(s, w_ref[0].astype(BF16)) + b_ref[0]


def _mod_call(cvec, ada_w, ada_b):
    depth, d, nd = ada_w.shape
    rows = cvec.shape[0]
    tn = d
    return pl.pallas_call(
        _mod_kernel,
        grid=(depth, nd // tn),
        in_specs=[
            pl.BlockSpec((rows, d), lambda l, j: (0, 0)),
            pl.BlockSpec((1, d, tn), lambda l, j: (l, 0, j)),
            pl.BlockSpec((1, 1, tn), lambda l, j: (l, 0, j)),
        ],
        out_specs=pl.BlockSpec((1, rows, tn), lambda l, j: (l, 0, j)),
        out_shape=jax.ShapeDtypeStruct((depth, rows, nd), F32),
        compiler_params=_params("parallel", "parallel"),
        name="adaln_mod",
    )(cvec, ada_w, ada_b.reshape(depth, 1, nd))


def _mod_rows(mod_ref, k0):
    return mod_ref[0, k0:k0 + 1, :], mod_ref[0, k0 + 1:k0 + 2, :], mod_ref[0, k0 + 2:k0 + 3, :]


def _ffn_rows(x, mod_ref, k0, nw_ref, wg_ref, wu_ref, wd_ref):
    shift, scale, gate = _mod_rows(mod_ref, k0)
    u = _norm_modulate(x, nw_ref[...], shift, scale).astype(BF16)
    g = _dot(u, wg_ref[...])
    up = _dot(u, wu_ref[...])
    a = (g * _sigmoid(g) * up).astype(BF16)
    return x + (0.5 * gate) * _dot(a, wd_ref[...])


def _subtiles(tm):
    sub = min(tm, SUBTILE)
    return [slice(r0, r0 + sub) for r0 in range(0, tm, sub)]


def _mod_spec(mod):
    nd = mod.shape[-1]
    if mod.shape[0] == 1:
        return pl.BlockSpec((1, N_MOD, nd), lambda b, i: (0, 0, 0))
    return pl.BlockSpec((1, N_MOD, nd), lambda b, i: (b, 0, 0))


def _row_tile(n, target):
    return min(n, target)


def _chunk_cumsum(x, reverse):
    n = x.shape[-1]
    x3 = x.reshape(GROUPS, SUBLANES, n)
    r8 = lax.broadcasted_iota(jnp.int32, (1, SUBLANES, n), 1)
    s = 1
    while s < SUBLANES:
        if reverse:
            x3 = x3 + jnp.where(r8 < SUBLANES - s, pltpu.roll(x3, SUBLANES - s, 1), 0.0)
        else:
            x3 = x3 + jnp.where(r8 >= s, pltpu.roll(x3, s, 1), 0.0)
        s *= 2
    edge = 0 if reverse else SUBLANES - 1
    totals = [x3[g, edge:edge + 1, :] for g in range(GROUPS)]
    order = range(GROUPS - 1, -1, -1) if reverse else range(GROUPS)
    out = [None] * GROUPS
    run = None
    for g in order:
        out[g] = x3[g] if run is None else x3[g] + run
        run = totals[g] if run is None else run + totals[g]
    return jnp.concatenate(out, axis=0)


def _inproj_rows(x, rows, mod_ref, nw_ref, w_ref, lbf_ref, lbb_ref, q_o, kf_o, cf_o, kb_o, cb_o, v_o, pv_o):
    hw, pw = HG_WIDTH, POOL_WIDTH
    u = _norm_modulate(x, nw_ref[...], mod_ref[0, 3:4, :], mod_ref[0, 4:5, :]).astype(BF16)
    z_all = _dot(u, w_ref[...])

    def proj(c0, width):
        return z_all[:, c0:c0 + width]

    q = proj(0, hw)
    q_o[0, rows, :] = q * _sigmoid(q)

    def gates(z, lb, k_o, c_o, reverse):
        f = lb + (1.0 - lb) * _sigmoid(z)
        k_o[0, rows, :] = 1.0 - f
        lf = jnp.log(f) * LOG2_E
        for r0 in range(0, x.shape[0], CHUNK):
            c_o[0, rows.start + r0:rows.start + r0 + CHUNK, :] = _chunk_cumsum(lf[r0:r0 + CHUNK], reverse)

    gates(proj(hw, hw), lbf_ref[...], kf_o, cf_o, False)
    gates(proj(2 * hw, hw), lbb_ref[...], kb_o, cb_o, True)
    v_o[0, rows, :] = proj(3 * hw, hw).astype(BF16)
    pv_o[0, rows, :] = proj(4 * hw, pw)


def _ffn_inproj_kernel(h_ref, mod_ref, nw0_ref, wg_ref, wu_ref, wd_ref, nw1_ref, w_ref, lbf_ref, lbb_ref,
                       h_o, *proj_outs):
    for rows in _subtiles(h_ref.shape[1]):
        h1 = _ffn_rows(h_ref[0, rows, :], mod_ref, 0, nw0_ref, wg_ref, wu_ref, wd_ref)
        h_o[0, rows, :] = h1
        _inproj_rows(h1, rows, mod_ref, nw1_ref, w_ref, lbf_ref, lbb_ref, *proj_outs)


def _ffn_inproj_call(h, mod, nw0, wg, wu, wd, nw1, w_main, lb_f, lb_b):
    b, n, d = h.shape
    f = wg.shape[1]
    cols = w_main.shape[1]
    tm = _row_tile(n, 512)
    hw, pw = HG_WIDTH, POOL_WIDTH

    def tok(width):
        return pl.BlockSpec((1, tm, width), lambda bi, i: (bi, i, 0))

    def out(width, dtype=F32):
        return jax.ShapeDtypeStruct((b, n, width), dtype)

    widths = [d, hw, hw, hw, hw, hw, hw, pw]
    dtypes = [F32, F32, F32, F32, F32, F32, BF16, F32]
    return pl.pallas_call(
        _ffn_inproj_kernel,
        grid=(b, n // tm),
        in_specs=[tok(d), _mod_spec(mod), _const_spec((1, d)), _const_spec((d, f)), _const_spec((d, f)),
                  _const_spec((f, d)), _const_spec((1, d)), _const_spec((d, cols)),
                  _const_spec((1, hw)), _const_spec((1, hw))],
        out_specs=[tok(w) for w in widths],
        out_shape=[out(w, t) for w, t in zip(widths, dtypes)],
        compiler_params=_params("parallel", "parallel"),
        name="ffn_in_proj",
    )(h, mod, nw0.reshape(1, d), wg, wu, wd, nw1.reshape(1, d), w_main, lb_f.reshape(1, hw),
      lb_b.reshape(1, hw))


def _level_masks():
    t = np.arange(CHUNK)[:, None]
    s = np.arange(CHUNK)[None, :]
    out = np.zeros((2, SCAN_LEVELS + 1, CHUNK, CHUNK), np.float32)
    for rev in (0, 1):
        for level in range(SCAN_LEVELS):
            bs = 1 << level
            same = (t >> (level + 1)) == (s >> (level + 1))
            tq = ((t & bs) == 0) if rev else ((t & bs) != 0)
            sk = ((s & bs) != 0) if rev else ((s & bs) == 0)
            out[rev, level] = same & tq & sk
        out[rev, SCAN_LEVELS] = t == s
    return np.concatenate([out, out], axis=-1)


def _neg_abs(x):
    bits = lax.bitcast_convert_type(x, jnp.uint32) | jnp.uint32(0x80000000)
    return lax.bitcast_convert_type(bits, F32)


def _level_operand(q3, k3, c3, level, reverse):
    width = q3.shape[-1]
    bs = 1 << level
    r8 = lax.broadcasted_iota(jnp.int32, (1, SUBLANES, width), 1)
    if bs < SUBLANES:
        is_q = ((r8 & bs) == 0) if reverse else ((r8 & bs) != 0)
        sel = jnp.where(is_q, q3, k3)
        if bs == 1:
            nb = pltpu.roll(c3, 7 if reverse else 1, 1)
            expo = jnp.where(is_q, c3 - nb, 0.0)
        else:
            off = bs if reverse else bs - 1
            anchor = c3[:, SUBLANES - 2 * bs + off:SUBLANES - 2 * bs + off + 1, :]
            for blk in range(SUBLANES // (2 * bs) - 2, -1, -1):
                row = blk * 2 * bs + off
                anchor = jnp.where(r8 < (blk + 1) * 2 * bs, c3[:, row:row + 1, :], anchor)
            expo = _neg_abs(c3 - anchor)
        return sel * jnp.exp2(expo)
    gbs = bs // SUBLANES
    sel, anchor = [], []
    for g in range(GROUPS):
        start = (g // (2 * gbs)) * (2 * gbs)
        is_q = ((g & gbs) == 0) if reverse else ((g & gbs) != 0)
        sel.append((q3 if is_q else k3)[g:g + 1])
        if reverse:
            anchor.append(c3[start + gbs:start + gbs + 1, 0:1, :])
        else:
            anchor.append(c3[start + gbs - 1:start + gbs, SUBLANES - 1:SUBLANES, :])
    sel = jnp.concatenate(sel, axis=0)
    anchor = jnp.concatenate(anchor, axis=0)
    return sel * jnp.exp2(_neg_abs(c3 - anchor))


def _block_diag(a, b):
    z = jnp.zeros_like(a)
    return jnp.concatenate([jnp.concatenate([a, z], axis=1), jnp.concatenate([z, b], axis=1)], axis=0)


def _scan_direction(q, k, c, v, mask_ref, st_ref, reverse):
    width = q.shape[-1]
    q3, k3, c3 = (a.reshape(GROUPS, SUBLANES, width) for a in (q, k, c))
    pairs = [(slice(lo, lo + HEAD_DIM), slice(lo + HEAD_DIM, lo + PAIR), slice(lo, lo + PAIR))
             for lo in range(0, width, PAIR)]

    def scores(lhs, rhs, level):
        return [_dot_nt(lhs[:, pr], _block_diag(rhs[:, ha], rhs[:, hb])) * mask_ref[level]
                for ha, hb, pr in pairs]

    qk = q * k
    lane = lax.broadcasted_iota(jnp.int32, (CHUNK, HEAD_DIM), 1)
    accs = []
    for ha, hb, _ in pairs:
        da = jnp.sum(qk[:, ha], axis=-1, keepdims=True)
        db = jnp.sum(qk[:, hb], axis=-1, keepdims=True)
        accs.append(jnp.where(lane < CHUNK, da, db) * mask_ref[SCAN_LEVELS])
    for level in range(SCAN_LEVELS):
        x = _level_operand(q3, k3, c3, level, reverse).reshape(CHUNK, width).astype(BF16)
        accs = [a + s for a, s in zip(accs, scores(x, x, level))]

    last = c[0:1, :] if reverse else c[CHUNK - 1:CHUNK, :]
    qe = (q * jnp.exp2(c)).astype(BF16)
    k_dec = (k * jnp.exp2(last - c)).astype(BF16)
    decay = jnp.exp2(last)
    outs = []
    for p, (ha, hb, _) in enumerate(pairs):
        intra = _dot(accs[p].astype(BF16), _block_diag(v[:, ha], v[:, hb]))
        inter = []
        for h, lanes in ((2 * p, ha), (2 * p + 1, hb)):
            st = st_ref[h]
            inter.append(_dot_nt(qe[:, lanes], st.astype(BF16)))
            st_ref[h] = st * decay[:, lanes] + _dot_tn(v[:, lanes], k_dec[:, lanes])
        outs.append(intra + jnp.concatenate(inter, axis=1))
    return jnp.concatenate(outs, axis=1)


def _scan_kernel(qf_ref, vf_ref, kf_ref, cf_ref, qb_ref, vb_ref, kb_ref, cb_ref, s0_ref, mask_ref,
                 of_ref, ob_ref, sfin_ref, st_ref, *, n_chunks):
    j = pl.program_id(1)

    @pl.when(j == 0)
    def _():
        st_ref[...] = s0_ref[0]

    def body(i, carry):
        rf = pl.ds(pl.multiple_of(i * CHUNK, CHUNK), CHUNK)
        of_ref[0, rf, :] = _scan_direction(qf_ref[0, rf, :], kf_ref[0, rf, :], cf_ref[0, rf, :],
                                           vf_ref[0, rf, :], mask_ref.at[0], st_ref.at[0], False)
        rb = pl.ds(pl.multiple_of((n_chunks - 1 - i) * CHUNK, CHUNK), CHUNK)
        ob_ref[0, rb, :] = _scan_direction(qb_ref[0, rb, :], kb_ref[0, rb, :], cb_ref[0, rb, :],
                                           vb_ref[0, rb, :], mask_ref.at[1], st_ref.at[1], True)
        return carry

    lax.fori_loop(0, n_chunks, body, 0, unroll=4)

    @pl.when(j == pl.num_programs(1) - 1)
    def _():
        sfin_ref[0] = st_ref[...]


def _scan_call(q, kf, cf, kb, cb, v, s0):
    b, n, hw = q.shape
    ts = _row_tile(n, 512)
    nb = n // ts
    fwd = pl.BlockSpec((1, ts, hw), lambda bi, j: (bi, j, 0))
    bwd = pl.BlockSpec((1, ts, hw), lambda bi, j: (bi, nb - 1 - j, 0))
    st_shape = (2, HEADS, HEAD_DIM, HEAD_DIM)
    st_spec = pl.BlockSpec((1,) + st_shape, lambda bi, j: (bi, 0, 0, 0, 0))
    masks = jnp.asarray(_level_masks())
    return pl.pallas_call(
        functools.partial(_scan_kernel, n_chunks=ts // CHUNK),
        grid=(b, nb),
        in_specs=[fwd, fwd, fwd, fwd, bwd, bwd, bwd, bwd, st_spec, _const_spec(masks.shape)],
        out_specs=[fwd, bwd, st_spec],
        out_shape=[jax.ShapeDtypeStruct((b, n, hw), F32), jax.ShapeDtypeStruct((b, n, hw), F32),
                   jax.ShapeDtypeStruct((b,) + st_shape, F32)],
        scratch_shapes=[pltpu.VMEM(st_shape, F32)],
        compiler_params=_params("parallel", "arbitrary"),
        name="gla_scan",
    )(q, v, kf, cf, q, v, kb, cb, s0, masks)


def _inv_count(shape, axis, length, w):
    i = lax.broadcasted_iota(jnp.int32, shape, axis) - POOL_PAD
    hi = jnp.minimum(i + (w - w // 2), length)
    lo = jnp.maximum(i - w // 2, 0)
    return 1.0 / jnp.maximum(hi - lo, 1).astype(F32)


def _pool_body(x_ref, pw_ref, sc_ref, o_ref, pad_ref, w, two_d):
    rows, width, gd = x_ref.shape[1:]
    wp = width + 2 * POOL_PAD
    rp = POOL_PAD if two_d else 0
    x = x_ref[0]
    pad_ref[...] = jnp.zeros(pad_ref.shape, F32)
    pad_ref[rp:rp + rows, POOL_PAD:POOL_PAD + width, :] = x
    t = pad_ref[...]
    t = t + pltpu.roll(t, 1, 1)
    k = 2
    while k < w:
        s = k // 2
        t = pltpu.roll(t, s, 1) + pltpu.roll(t, wp - s, 1)
        k *= 2
    t = t * _inv_count((1, wp, gd), 1, width, w)
    if two_d:
        n = t.shape[0]
        t = t[0:n - 1] + t[1:n]
        first = 1
        k = 2
        while k < w:
            n = t.shape[0]
            t = t[0:n - k] + t[k:n]
            first += k // 2
            k *= 2
        t = t[POOL_PAD - first:POOL_PAD - first + rows]
        inv_r = _inv_count((rows + 2 * POOL_PAD, 1, gd), 0, rows, w)[POOL_PAD:POOL_PAD + rows]
        t = t * inv_r
    m = t[:, POOL_PAD:POOL_PAD + width, :]
    dlt = (m - x).reshape(rows * width, gd).astype(BF16)
    y = _dot(dlt, pw_ref[0])
    o_ref[0] = (y * sc_ref[...]).astype(o_ref.dtype)


def _pool_kernel(x_ref, pw_ref, sc_ref, o_ref, pad_ref, *, two_d):
    g = pl.program_id(0)
    for gi, w in enumerate(POOL_WINDOWS):
        @pl.when(g == gi)
        def _(w=w):
            _pool_body(x_ref, pw_ref, sc_ref, o_ref, pad_ref, w, two_d)


def _pool_call(pv, pool_w, pool_scale, two_d):
    b, n, pw = pv.shape
    width = GRID_W if two_d else n
    rows = n // width
    groups = len(POOL_WINDOWS)
    pad_rows = rows + 2 * POOL_PAD if two_d else rows
    return pl.pallas_call(
        functools.partial(_pool_kernel, two_d=two_d),
        grid=(groups, b),
        in_specs=[
            pl.BlockSpec((1, rows, width, POOL_GD), lambda g, bi: (bi, 0, 0, g)),
            pl.BlockSpec((1, POOL_GD, POOL_GD), lambda g, bi: (g, 0, 0)),
            pl.BlockSpec((1, POOL_GD), lambda g, bi: (0, g)),
        ],
        out_specs=pl.BlockSpec((1, n, POOL_GD), lambda g, bi: (bi, 0, g)),
        out_shape=jax.ShapeDtypeStruct((b, n, pw), BF16),
        scratch_shapes=[pltpu.VMEM((pad_rows, width + 2 * POOL_PAD, POOL_GD), F32)],
        compiler_params=_params("parallel", "parallel"),
        name="pool_mixer",
    )(pv.reshape(b, rows, width, pw), pool_w, pool_scale.reshape(1, pw))


def _merge_rows(x, o, p, mod_ref, nw_ref, hnw_ref, wgate_ref, wa_ref, wb_ref, wo_ref):
    d = x.shape[-1]
    hw = HG_WIDTH
    u = _norm_modulate(x, nw_ref[...], mod_ref[0, 3:4, :], mod_ref[0, 4:5, :]).astype(BF16)
    parts = []
    for h in range(HEADS):
        oh = o[:, h * HEAD_DIM:(h + 1) * HEAD_DIM]
        ms = jnp.mean(oh * oh, axis=-1, keepdims=True)
        parts.append(oh * lax.rsqrt(ms + EPS))
    on = jnp.concatenate(parts, axis=-1) * hnw_ref[...]
    g = _dot(u, wgate_ref[:, 0:hw])
    a = (on * (g * _sigmoid(g))).astype(BF16)
    ga = _sigmoid(_dot(u, wgate_ref[:, hw:hw + d]))
    gb = _sigmoid(_dot(u, wgate_ref[:, hw + d:hw + 2 * d]))
    merged = ga * _dot(a, wa_ref[...]) + gb * _dot(p, wb_ref[...])
    y = _dot(merged.astype(BF16), wo_ref[...])
    return x + mod_ref[0, 5:6, :] * y


def _merge_ffn_kernel(*refs, final):
    (of_ref, ob_ref, p_ref, h_ref, mod_ref, nw1_ref, hnw_ref, wgate_ref, wa_ref, wb_ref, wo_ref,
     nw2_ref, wg_ref, wu_ref, wd_ref) = refs[:15]
    fnw_ref = refs[15] if final else None
    o_ref = refs[-1]
    for rows in _subtiles(h_ref.shape[1]):
        o = of_ref[0, rows, :] + ob_ref[0, rows, :]
        hm = _merge_rows(h_ref[0, rows, :], o, p_ref[0, rows, :], mod_ref, nw1_ref, hnw_ref,
                         wgate_ref, wa_ref, wb_ref, wo_ref)
        out = _ffn_rows(hm, mod_ref, 6, nw2_ref, wg_ref, wu_ref, wd_ref)
        if final:
            ms = jnp.mean(out * out, axis=-1, keepdims=True)
            out = out * lax.rsqrt(ms + EPS) * fnw_ref[...]
        o_ref[0, rows, :] = out


def _merge_ffn_call(o_f, o_b, p, h, mod, nw1, hnw, w_gate, w_a, w_b, w_out, nw2, wg, wu, wd, final_nw=None):
    b, n, d = h.shape
    f = wg.shape[1]
    hw, pw = HG_WIDTH, POOL_WIDTH
    tm = _row_tile(n, 512)
    final = final_nw is not None

    def tok(width):
        return pl.BlockSpec((1, tm, width), lambda bi, i: (bi, i, 0))

    in_specs = [tok(hw), tok(hw), tok(pw), tok(d), _mod_spec(mod), _const_spec((1, d)),
                _const_spec((1, hw)), _const_spec(w_gate.shape), _const_spec((hw, d)),
                _const_spec((pw, d)), _const_spec((d, d)),
                _const_spec((1, d)), _const_spec((d, f)), _const_spec((d, f)), _const_spec((f, d))]
    args = [o_f, o_b, p, h, mod, nw1.reshape(1, d), hnw.reshape(1, hw), w_gate, w_a, w_b, w_out,
            nw2.reshape(1, d), wg, wu, wd]
    if final:
        in_specs.append(_const_spec((1, d)))
        args.append(final_nw.reshape(1, d))
    return pl.pallas_call(
        functools.partial(_merge_ffn_kernel, final=final),
        grid=(b, n // tm),
        in_specs=in_specs,
        out_specs=tok(d),
        out_shape=jax.ShapeDtypeStruct((b, n, d), F32),
        compiler_params=_params("parallel", "parallel"),
        name="merge_ffn",
    )(*args)


def kernel(x, c, ctx, c_ctx, ada_w, ada_b, norm_w, ffn_wg, ffn_wu, ffn_wd, w_in, lower_bounds, hg_norm_w,
           pool_w, pool_scale, w_branch_a, w_branch_b, w_out, final_norm_w):
    b, n, d = x.shape
    depth = ada_w.shape[0]
    hw, pw = HG_WIDTH, POOL_WIDTH
    assert n % GRID_W == 0 and n % CHUNK == 0 and ctx.shape[1] % CHUNK == 0

    n_rows = -(-(b + 1) // 8) * 8
    cvec = jnp.zeros((n_rows, d), F32).at[:b].set(c).at[b].set(c_ctx)
    mods = _mod_call(cvec, ada_w, ada_b).reshape(depth, n_rows, N_MOD, d)

    sm = jax.nn.softmax(lower_bounds.astype(F32), axis=1)
    lbs = jnp.cumsum(sm, axis=1) - sm[:, :1]

    wg, wu, wd = ffn_wg.astype(BF16), ffn_wu.astype(BF16), ffn_wd.astype(BF16)
    w_in_b = w_in.astype(BF16)
    w_main = jnp.concatenate([w_in_b[:, :, :4 * hw], w_in_b[:, :, 5 * hw:5 * hw + pw]], axis=-1)
    w_gate = jnp.concatenate([w_in_b[:, :, 4 * hw:5 * hw], w_in_b[:, :, 5 * hw + pw:]], axis=-1)
    w_a, w_b, w_o = w_branch_a.astype(BF16), w_branch_b.astype(BF16), w_out.astype(BF16)
    pool_w_b = pool_w.astype(BF16)

    zeros_state = jnp.zeros((b, 2, HEADS, HEAD_DIM, HEAD_DIM), F32)
    h, hc = x, ctx
    for l in range(depth):
        last = l == depth - 1
        mod_lat, mod_ctx = mods[l, :b], mods[l, b:b + 1]
        ffn1 = (norm_w[l, 0], wg[l, 0], wu[l, 0], wd[l, 0])
        ffn2 = (norm_w[l, 2], wg[l, 1], wu[l, 1], wd[l, 1])
        proj = (norm_w[l, 1], w_main[l], lbs[0, l], lbs[1, l])
        mix = (norm_w[l, 1], hg_norm_w[l], w_gate[l], w_a[l], w_b[l], w_o[l])

        hc, qc, kfc, cfc, kbc, cbc, vc, pvc = _ffn_inproj_call(hc, mod_ctx, *ffn1, *proj)
        ofc, obc, s_ctx = _scan_call(qc, kfc, cfc, kbc, cbc, vc, zeros_state)

        h, q, kf, cf, kb, cb, v, pv = _ffn_inproj_call(h, mod_lat, *ffn1, *proj)
        o_f, o_b, _ = _scan_call(q, kf, cf, kb, cb, v, s_ctx)
        p = _pool_call(pv, pool_w_b[l], pool_scale[l], True)
        h = _merge_ffn_call(o_f, o_b, p, h, mod_lat, *mix, *ffn2, final_nw=final_norm_w if last else None)

        if not last:
            pc = _pool_call(pvc, pool_w_b[l], pool_scale[l], False)
            hc = _merge_ffn_call(ofc, obc, pc, hc, mod_ctx, *mix, *ffn2)
    return h
```

```python
import functools

import numpy as np
import jax
import jax.numpy as jnp
from jax import lax
from jax.experimental import pallas as pl
from jax.experimental.pallas import tpu as pltpu

EPS = 1e-6
N_MOD = 9
HEADS = 4
HEAD_DIM = 128
HG_WIDTH = HEADS * HEAD_DIM
POOL_WINDOWS = (2, 4, 8, 16)
POOL_GD = 128
POOL_WIDTH = POOL_GD * len(POOL_WINDOWS)
GRID_W = 64
CHUNK = 64
SUBLANES = 8
GROUPS = CHUNK // SUBLANES
SCAN_LEVELS = 6
PAIR = 2 * HEAD_DIM
POOL_PAD = 8
SUBTILE = 256
LOG2_E = 1.4426950408889634

VMEM_LIMIT_BYTES = 56 * 1024 * 1024

F32 = jnp.float32
BF16 = jnp.bfloat16


def _params(*sem):
    return pltpu.CompilerParams(dimension_semantics=sem, vmem_limit_bytes=VMEM_LIMIT_BYTES)


def _const_spec(shape):
    nd = len(shape)
    return pl.BlockSpec(shape, lambda *_: (0,) * nd, pipeline_mode=pl.Buffered(1))


def _sigmoid(x):
    return jax.nn.sigmoid(x)


def _dot(a, b):
    return jnp.dot(a, b, preferred_element_type=F32)


def _dot_nt(a, b):
    return lax.dot_general(a, b, (((1,), (1,)), ((), ())), preferred_element_type=F32)


def _dot_tn(a, b):
    return lax.dot_general(a, b, (((0,), (0,)), ((), ())), preferred_element_type=F32)


def _norm_modulate(x, nw, shift, scale):
    ms = jnp.mean(x * x, axis=-1, keepdims=True)
    y = x * lax.rsqrt(ms + EPS) * nw
    return y * (1.0 + scale) + shift


def _mod_kernel(c_ref, w_ref, b_ref, o_ref):
    c = c_ref[...]
    s = (c * _sigmoid(c)).astype(BF16)
    o_ref[0] = _dot(s, w_ref[0].astype(BF16)) + b_ref[0]


def _mod_call(cvec, ada_w, ada_b):
    depth, d, nd = ada_w.shape
    rows = cvec.shape[0]
    tn = d
    return pl.pallas_call(
        _mod_kernel,
        grid=(depth, nd // tn),
        in_specs=[
            pl.BlockSpec((rows, d), lambda l, j: (0, 0)),
            pl.BlockSpec((1, d, tn), lambda l, j: (l, 0, j)),
            pl.BlockSpec((1, 1, tn), lambda l, j: (l, 0, j)),
        ],
        out_specs=pl.BlockSpec((1, rows, tn), lambda l, j: (l, 0, j)),
        out_shape=jax.ShapeDtypeStruct((depth, rows, nd), F32),
        compiler_params=_params("parallel", "parallel"),
        name="adaln_mod",
    )(cvec, ada_w, ada_b.reshape(depth, 1, nd))


def _mod_rows(mod_ref, k0):
    return mod_ref[0, k0:k0 + 1, :], mod_ref[0, k0 + 1:k0 + 2, :], mod_ref[0, k0 + 2:k0 + 3, :]


def _ffn_rows(x, mod_ref, k0, nw_ref, wg_ref, wu_ref, wd_ref):
    shift, scale, gate = _mod_rows(mod_ref, k0)
    u = _norm_modulate(x, nw_ref[...], shift, scale).astype(BF16)
    g = _dot(u, wg_ref[...])
    up = _dot(u, wu_ref[...])
    a = (g * _sigmoid(g) * up).astype(BF16)
    return x + (0.5 * gate) * _dot(a, wd_ref[...])


def _subtiles(tm):
    sub = min(tm, SUBTILE)
    return [slice(r0, r0 + sub) for r0 in range(0, tm, sub)]


def _mod_spec(mod):
    nd = mod.shape[-1]
    if mod.shape[0] == 1:
        return pl.BlockSpec((1, N_MOD, nd), lambda b, i: (0, 0, 0))
    return pl.BlockSpec((1, N_MOD, nd), lambda b, i: (b, 0, 0))


def _row_tile(n, target):
    return min(n, target)


def _chunk_cumsum(x, reverse):
    n = x.shape[-1]
    x3 = x.reshape(GROUPS, SUBLANES, n)
    r8 = lax.broadcasted_iota(jnp.int32, (1, SUBLANES, n), 1)
    s = 1
    while s < SUBLANES:
        if reverse:
            x3 = x3 + jnp.where(r8 < SUBLANES - s, pltpu.roll(x3, SUBLANES - s, 1), 0.0)
        else:
            x3 = x3 + jnp.where(r8 >= s, pltpu.roll(x3, s, 1), 0.0)
        s *= 2
    edge = 0 if reverse else SUBLANES - 1
    totals = [x3[g, edge:edge + 1, :] for g in range(GROUPS)]
    order = range(GROUPS - 1, -1, -1) if reverse else range(GROUPS)
    out = [None] * GROUPS
    run = None
    for g in order:
        out[g] = x3[g] if run is None else x3[g] + run
        run = totals[g] if run is None else run + totals[g]
    return jnp.concatenate(out, axis=0)


def _inproj_epilogue(z_all, rows, lb_f, lb_b, q_o, kf_o, cf_o, kb_o, cb_o, v_o, pv_o):
    hw, pw = HG_WIDTH, POOL_WIDTH

    def proj(c0, width):
        return z_all[:, c0:c0 + width]

    q = proj(0, hw)
    q_o[0, rows, :] = (q * _sigmoid(q)).astype(q_o.dtype)

    def gates(z, lb, k_o, c_o, reverse):
        f = lb + (1.0 - lb) * _sigmoid(z)
        k_o[0, rows, :] = (1.0 - f).astype(k_o.dtype)
        lf = jnp.log(f) * LOG2_E
        for r0 in range(0, z_all.shape[0], CHUNK):
            c_o[0, rows.start + r0:rows.start + r0 + CHUNK, :] = _chunk_cumsum(lf[r0:r0 + CHUNK], reverse)

    gates(proj(hw, hw), lb_f, kf_o, cf_o, False)
    gates(proj(2 * hw, hw), lb_b, kb_o, cb_o, True)
    v_o[0, rows, :] = proj(3 * hw, hw).astype(BF16)
    pv_o[0, rows, :] = proj(4 * hw, pw)


def _ffn_inproj_kernel(h_ref, mod_ref, nw0_ref, wg_ref, wu_ref, wd_ref, nw1_ref, w_ref, lbf_ref, lbb_ref,
                       h_o, *proj_outs):
    for rows in _subtiles(h_ref.shape[1]):
        h1 = _ffn_rows(h_ref[0, rows, :], mod_ref, 0, nw0_ref, wg_ref, wu_ref, wd_ref)
        h_o[0, rows, :] = h1
        u = _norm_modulate(h1, nw1_ref[...], mod_ref[0, 3:4, :], mod_ref[0, 4:5, :]).astype(BF16)
        _inproj_epilogue(_dot(u, w_ref[...]), rows, lbf_ref[...], lbb_ref[...], *proj_outs)


def _ffn_inproj_call(h, mod, nw0, wg, wu, wd, nw1, w_main, lb_f, lb_b):
    b, n, d = h.shape
    f = wg.shape[1]
    cols = w_main.shape[1]
    tm = _row_tile(n, 512)
    hw, pw = HG_WIDTH, POOL_WIDTH

    def tok(width):
        return pl.BlockSpec((1, tm, width), lambda bi, i: (bi, i, 0))

    def out(width, dtype=F32):
        return jax.ShapeDtypeStruct((b, n, width), dtype)

    widths = [d, hw, hw, hw, hw, hw, hw, pw]
    dtypes = [F32, BF16, BF16, F32, BF16, F32, BF16, F32]
    return pl.pallas_call(
        _ffn_inproj_kernel,
        grid=(b, n // tm),
        in_specs=[tok(d), _mod_spec(mod), _const_spec((1, d)), _const_spec((d, f)), _const_spec((d, f)),
                  _const_spec((f, d)), _const_spec((1, d)), _const_spec((d, cols)),
                  _const_spec((1, hw)), _const_spec((1, hw))],
        out_specs=[tok(w) for w in widths],
        out_shape=[out(w, t) for w, t in zip(widths, dtypes)],
        compiler_params=_params("parallel", "parallel"),
        name="ffn_in_proj",
    )(h, mod, nw0.reshape(1, d), wg, wu, wd, nw1.reshape(1, d), w_main, lb_f.reshape(1, hw),
      lb_b.reshape(1, hw))


def _level_masks():
    t = np.arange(CHUNK)[:, None]
    s = np.arange(CHUNK)[None, :]
    out = np.zeros((2, SCAN_LEVELS + 1, CHUNK, CHUNK), np.float32)
    for rev in (0, 1):
        for level in range(SCAN_LEVELS):
            bs = 1 << level
            same = (t >> (level + 1)) == (s >> (level + 1))
            tq = ((t & bs) == 0) if rev else ((t & bs) != 0)
            sk = ((s & bs) != 0) if rev else ((s & bs) == 0)
            out[rev, level] = same & tq & sk
        out[rev, SCAN_LEVELS] = t == s
    return np.concatenate([out, out], axis=-1)


def _neg_abs(x):
    bits = lax.bitcast_convert_type(x, jnp.uint32) | jnp.uint32(0x80000000)
    return lax.bitcast_convert_type(bits, F32)


def _level_operand(q3, k3, c3, level, reverse):
    width = q3.shape[-1]
    bs = 1 << level
    r8 = lax.broadcasted_iota(jnp.int32, (1, SUBLANES, width), 1)
    if bs < SUBLANES:
        is_q = ((r8 & bs) == 0) if reverse else ((r8 & bs) != 0)
        sel = jnp.where(is_q, q3, k3)
        if bs == 1:
            nb = pltpu.roll(c3, 7 if reverse else 1, 1)
            expo = jnp.where(is_q, c3 - nb, 0.0)
        else:
            off = bs if reverse else bs - 1
            anchor = c3[:, SUBLANES - 2 * bs + off:SUBLANES - 2 * bs + off + 1, :]
            for blk in range(SUBLANES // (2 * bs) - 2, -1, -1):
                row = blk * 2 * bs + off
                anchor = jnp.where(r8 < (blk + 1) * 2 * bs, c3[:, row:row + 1, :], anchor)
            expo = _neg_abs(c3 - anchor)
        return sel * jnp.exp2(expo)
    gbs = bs // SUBLANES
    sel, anchor = [], []
    for g in range(GROUPS):
        start = (g // (2 * gbs)) * (2 * gbs)
        is_q = ((g & gbs) == 0) if reverse else ((g & gbs) != 0)
        sel.append((q3 if is_q else k3)[g:g + 1])
        if reverse:
            anchor.append(c3[start + gbs:start + gbs + 1, 0:1, :])
        else:
            anchor.append(c3[start + gbs - 1:start + gbs, SUBLANES - 1:SUBLANES, :])
    sel = jnp.concatenate(sel, axis=0)
    anchor = jnp.concatenate(anchor, axis=0)
    return sel * jnp.exp2(_neg_abs(c3 - anchor))


def _block_diag(a, b):
    z = jnp.zeros_like(a)
    return jnp.concatenate([jnp.concatenate([a, z], axis=1), jnp.concatenate([z, b], axis=1)], axis=0)


def _scan_direction(q, k, c, v, mask_ref, st_ref, reverse):
    width = q.shape[-1]
    q, k = q.astype(F32), k.astype(F32)
    q3, k3, c3 = (a.reshape(GROUPS, SUBLANES, width) for a in (q, k, c))
    pairs = [(slice(lo, lo + HEAD_DIM), slice(lo + HEAD_DIM, lo + PAIR), slice(lo, lo + PAIR))
             for lo in range(0, width, PAIR)]

    def scores(lhs, rhs, level):
        return [_dot_nt(lhs[:, pr], _block_diag(rhs[:, ha], rhs[:, hb])) * mask_ref[level]
                for ha, hb, pr in pairs]

    qk = q * k
    lane = lax.broadcasted_iota(jnp.int32, (CHUNK, HEAD_DIM), 1)
    accs = []
    for ha, hb, _ in pairs:
        da = jnp.sum(qk[:, ha], axis=-1, keepdims=True)
        db = jnp.sum(qk[:, hb], axis=-1, keepdims=True)
        accs.append(jnp.where(lane < CHUNK, da, db) * mask_ref[SCAN_LEVELS])
    for level in range(SCAN_LEVELS):
        x = _level_operand(q3, k3, c3, level, reverse).reshape(CHUNK, width).astype(BF16)
        accs = [a + s for a, s in zip(accs, scores(x, x, level))]

    last = c[0:1, :] if reverse else c[CHUNK - 1:CHUNK, :]
    qe = (q * jnp.exp2(c)).astype(BF16)
    k_dec = (k * jnp.exp2(last - c)).astype(BF16)
    decay = jnp.exp2(last)
    outs = []
    for p, (ha, hb, _) in enumerate(pairs):
        intra = _dot(accs[p].astype(BF16), _block_diag(v[:, ha], v[:, hb]))
        inter = []
        for h, lanes in ((2 * p, ha), (2 * p + 1, hb)):
            st = st_ref[h]
            inter.append(_dot_nt(qe[:, lanes], st.astype(BF16)))
            st_ref[h] = st * decay[:, lanes] + _dot_tn(v[:, lanes], k_dec[:, lanes])
        outs.append(intra + jnp.concatenate(inter, axis=1))
    return jnp.concatenate(outs, axis=1)


def _scan_kernel(qf_ref, vf_ref, kf_ref, cf_ref, qb_ref, vb_ref, kb_ref, cb_ref, s0_ref, mask_ref,
                 of_ref, ob_ref, sfin_ref, st_ref, *, n_chunks):
    j = pl.program_id(1)

    @pl.when(j == 0)
    def _():
        st_ref[...] = s0_ref[0]

    def body(i, carry):
        rf = pl.ds(pl.multiple_of(i * CHUNK, CHUNK), CHUNK)
        of_ref[0, rf, :] = _scan_direction(qf_ref[0, rf, :], kf_ref[0, rf, :], cf_ref[0, rf, :],
                                           vf_ref[0, rf, :], mask_ref.at[0], st_ref.at[0], False)
        rb = pl.ds(pl.multiple_of((n_chunks - 1 - i) * CHUNK, CHUNK), CHUNK)
        ob_ref[0, rb, :] = _scan_direction(qb_ref[0, rb, :], kb_ref[0, rb, :], cb_ref[0, rb, :],
                                           vb_ref[0, rb, :], mask_ref.at[1], st_ref.at[1], True)
        return carry

    lax.fori_loop(0, n_chunks, body, 0, unroll=4)

    @pl.when(j == pl.num_programs(1) - 1)
    def _():
        sfin_ref[0] = st_ref[...]


def _scan_call(q, kf, cf, kb, cb, v, s0):
    b, n, hw = q.shape
    ts = _row_tile(n, 1024)
    nb = n // ts
    fwd = pl.BlockSpec((1, ts, hw), lambda bi, j: (bi, j, 0))
    bwd = pl.BlockSpec((1, ts, hw), lambda bi, j: (bi, nb - 1 - j, 0))
    st_shape = (2, HEADS, HEAD_DIM, HEAD_DIM)
    st_spec = pl.BlockSpec((1,) + st_shape, lambda bi, j: (bi, 0, 0, 0, 0))
    masks = jnp.asarray(_level_masks())
    return pl.pallas_call(
        functools.partial(_scan_kernel, n_chunks=ts // CHUNK),
        grid=(b, nb),
        in_specs=[fwd, fwd, fwd, fwd, bwd, bwd, bwd, bwd, st_spec, _const_spec(masks.shape)],
        out_specs=[fwd, bwd, st_spec],
        out_shape=[jax.ShapeDtypeStruct((b, n, hw), F32), jax.ShapeDtypeStruct((b, n, hw), F32),
                   jax.ShapeDtypeStruct((b,) + st_shape, F32)],
        scratch_shapes=[pltpu.VMEM(st_shape, F32)],
        compiler_params=_params("parallel", "arbitrary"),
        name="gla_scan",
    )(q, v, kf, cf, q, v, kb, cb, s0, masks)


def _inv_count(shape, axis, length, w):
    i = lax.broadcasted_iota(jnp.int32, shape, axis) - POOL_PAD
    hi = jnp.minimum(i + (w - w // 2), length)
    lo = jnp.maximum(i - w // 2, 0)
    return 1.0 / jnp.maximum(hi - lo, 1).astype(F32)


def _pool_body(x_ref, pw_ref, sc_ref, o_ref, pad_ref, w, two_d):
    rows, width, gd = x_ref.shape[1:]
    wp = width + 2 * POOL_PAD
    x = x_ref[0]
    t = x
    if two_d:
        z = jnp.zeros((POOL_PAD, width, gd), F32)
        t = jnp.concatenate([z, x, z], axis=0)
        n = t.shape[0]
        t = t[0:n - 1] + t[1:n]
        first = 1
        k = 2
        while k < w:
            n = t.shape[0]
            t = t[0:n - k] + t[k:n]
            first += k // 2
            k *= 2
        t = t[POOL_PAD - first:POOL_PAD - first + rows]
        t = t * _inv_count((rows + 2 * POOL_PAD, 1, gd), 0, rows, w)[POOL_PAD:POOL_PAD + rows]
    halo = jnp.zeros((rows, POOL_PAD, gd), F32)
    pad_ref[:, 0:POOL_PAD, :] = halo
    pad_ref[:, POOL_PAD + width:wp, :] = halo
    pad_ref[:, POOL_PAD:POOL_PAD + width, :] = t
    t = pad_ref[...]
    k = 1
    while k < w:
        t = t + pltpu.roll(t, k, 1)
        k *= 2
    if w > 2:
        t = pltpu.roll(t, wp - (w // 2 - 1), 1)
    t = t * _inv_count((1, wp, gd), 1, width, w)
    m = t[:, POOL_PAD:POOL_PAD + width, :]
    dlt = (m - x).reshape(rows * width, gd).astype(BF16)
    y = _dot(dlt, pw_ref[0])
    o_ref[0] = (y * sc_ref[...]).astype(o_ref.dtype)


def _pool_kernel(x_ref, pw_ref, sc_ref, o_ref, pad_ref, *, two_d):
    g = pl.program_id(0)
    for gi, w in enumerate(POOL_WINDOWS):
        @pl.when(g == gi)
        def _(w=w):
            _pool_body(x_ref, pw_ref, sc_ref, o_ref, pad_ref, w, two_d)


def _pool_call(pv, pool_w, pool_scale, two_d):
    b, n, pw = pv.shape
    width = GRID_W if two_d else n
    rows = n // width
    groups = len(POOL_WINDOWS)
    return pl.pallas_call(
        functools.partial(_pool_kernel, two_d=two_d),
        grid=(groups, b),
        in_specs=[
            pl.BlockSpec((1, rows, width, POOL_GD), lambda g, bi: (bi, 0, 0, g)),
            pl.BlockSpec((1, POOL_GD, POOL_GD), lambda g, bi: (g, 0, 0)),
            pl.BlockSpec((1, POOL_GD), lambda g, bi: (0, g)),
        ],
        out_specs=pl.BlockSpec((1, n, POOL_GD), lambda g, bi: (bi, 0, g)),
        out_shape=jax.ShapeDtypeStruct((b, n, pw), BF16),
        scratch_shapes=[pltpu.VMEM((rows, width + 2 * POOL_PAD, POOL_GD), F32)],
        compiler_params=_params("parallel", "parallel"),
        name="pool_mixer",
    )(pv.reshape(b, rows, width, pw), pool_w, pool_scale.reshape(1, pw))


def _merge_rows(x, o, p, mod_ref, nw_ref, hnw_ref, wgate_ref, wa_ref, wb_ref, wo_ref):
    d = x.shape[-1]
    hw = HG_WIDTH
    u = _norm_modulate(x, nw_ref[...], mod_ref[0, 3:4, :], mod_ref[0, 4:5, :]).astype(BF16)
    parts = []
    for h in range(HEADS):
        oh = o[:, h * HEAD_DIM:(h + 1) * HEAD_DIM]
        ms = jnp.mean(oh * oh, axis=-1, keepdims=True)
        parts.append(oh * lax.rsqrt(ms + EPS))
    on = jnp.concatenate(parts, axis=-1) * hnw_ref[...]
    g = _dot(u, wgate_ref[:, 0:hw])
    a = (on * (g * _sigmoid(g))).astype(BF16)
    ga = _sigmoid(_dot(u, wgate_ref[:, hw:hw + d]))
    gb = _sigmoid(_dot(u, wgate_ref[:, hw + d:hw + 2 * d]))
    merged = ga * _dot(a, wa_ref[...]) + gb * _dot(p, wb_ref[...])
    y = _dot(merged.astype(BF16), wo_ref[...])
    return x + mod_ref[0, 5:6, :] * y


def _merge_ffn_kernel(*refs, final):
    (of_ref, ob_ref, p_ref, h_ref, mod_ref, nw1_ref, hnw_ref, wgate_ref, wa_ref, wb_ref, wo_ref,
     nw2_ref, wg_ref, wu_ref, wd_ref) = refs[:15]
    fnw_ref = refs[15] if final else None
    o_ref = refs[-1]
    for rows in _subtiles(h_ref.shape[1]):
        o = of_ref[0, rows, :] + ob_ref[0, rows, :]
        hm = _merge_rows(h_ref[0, rows, :], o, p_ref[0, rows, :], mod_ref, nw1_ref, hnw_ref,
                         wgate_ref, wa_ref, wb_ref, wo_ref)
        out = _ffn_rows(hm, mod_ref, 6, nw2_ref, wg_ref, wu_ref, wd_ref)
        if final:
            ms = jnp.mean(out * out, axis=-1, keepdims=True)
            out = out * lax.rsqrt(ms + EPS) * fnw_ref[...]
        o_ref[0, rows, :] = out


def _merge_ffn_call(o_f, o_b, p, h, mod, nw1, hnw, w_gate, w_a, w_b, w_out, nw2, wg, wu, wd, final_nw=None):
    b, n, d = h.shape
    f = wg.shape[1]
    hw, pw = HG_WIDTH, POOL_WIDTH
    tm = _row_tile(n, 512)
    final = final_nw is not None

    def tok(width):
        return pl.BlockSpec((1, tm, width), lambda bi, i: (bi, i, 0))

    in_specs = [tok(hw), tok(hw), tok(pw), tok(d), _mod_spec(mod), _const_spec((1, d)),
                _const_spec((1, hw)), _const_spec(w_gate.shape), _const_spec((hw, d)),
                _const_spec((pw, d)), _const_spec((d, d)),
                _const_spec((1, d)), _const_spec((d, f)), _const_spec((d, f)), _const_spec((f, d))]
    args = [o_f, o_b, p, h, mod, nw1.reshape(1, d), hnw.reshape(1, hw), w_gate, w_a, w_b, w_out,
            nw2.reshape(1, d), wg, wu, wd]
    if final:
        in_specs.append(_const_spec((1, d)))
        args.append(final_nw.reshape(1, d))
    return pl.pallas_call(
        functools.partial(_merge_ffn_kernel, final=final),
        grid=(b, n // tm),
        in_specs=in_specs,
        out_specs=tok(d),
        out_shape=jax.ShapeDtypeStruct((b, n, d), F32),
        compiler_params=_params("parallel", "parallel"),
        name="merge_ffn",
    )(*args)


def kernel(x, c, ctx, c_ctx, ada_w, ada_b, norm_w, ffn_wg, ffn_wu, ffn_wd, w_in, lower_bounds, hg_norm_w,
           pool_w, pool_scale, w_branch_a, w_branch_b, w_out, final_norm_w):
    b, n, d = x.shape
    depth = ada_w.shape[0]
    hw, pw = HG_WIDTH, POOL_WIDTH
    assert n % GRID_W == 0 and n % CHUNK == 0 and ctx.shape[1] % CHUNK == 0

    n_rows = -(-(b + 1) // 8) * 8
    cvec = jnp.zeros((n_rows, d), F32).at[:b].set(c).at[b].set(c_ctx)
    mods = _mod_call(cvec, ada_w, ada_b).reshape(depth, n_rows, N_MOD, d)

    sm = jax.nn.softmax(lower_bounds.astype(F32), axis=1)
    lbs = jnp.cumsum(sm, axis=1) - sm[:, :1]

    wg, wu, wd = ffn_wg.astype(BF16), ffn_wu.astype(BF16), ffn_wd.astype(BF16)
    w_in_b = w_in.astype(BF16)
    w_main = jnp.concatenate([w_in_b[:, :, :4 * hw], w_in_b[:, :, 5 * hw:5 * hw + pw]], axis=-1)
    w_gate = jnp.concatenate([w_in_b[:, :, 4 * hw:5 * hw], w_in_b[:, :, 5 * hw + pw:]], axis=-1)
    w_a, w_b, w_o = w_branch_a.astype(BF16), w_branch_b.astype(BF16), w_out.astype(BF16)
    pool_w_b = pool_w.astype(BF16)

    zeros_state = jnp.zeros((b, 2, HEADS, HEAD_DIM, HEAD_DIM), F32)
    h, hc = x, ctx
    for l in range(depth):
        last = l == depth - 1
        mod_lat, mod_ctx = mods[l, :b], mods[l, b:b + 1]
        ffn1 = (norm_w[l, 0], wg[l, 0], wu[l, 0], wd[l, 0])
        ffn2 = (norm_w[l, 2], wg[l, 1], wu[l, 1], wd[l, 1])
        proj = (norm_w[l, 1], w_main[l], lbs[0, l], lbs[1, l])
        mix = (norm_w[l, 1], hg_norm_w[l], w_gate[l], w_a[l], w_b[l], w_o[l])

        hc, qc, kfc, cfc, kbc, cbc, vc, pvc = _ffn_inproj_call(hc, mod_ctx, *ffn1, *proj)
        ofc, obc, s_ctx = _scan_call(qc, kfc, cfc, kbc, cbc, vc, zeros_state)

        h, q, kf, cf, kb, cb, v, pv = _ffn_inproj_call(h, mod_lat, *ffn1, *proj)
        o_f, o_b, _ = _scan_call(q, kf, cf, kb, cb, v, s_ctx)
        p = _pool_call(pv, pool_w_b[l], pool_scale[l], True)
        h = _merge_ffn_call(o_f, o_b, p, h, mod_lat, *mix, *ffn2, final_nw=final_norm_w if last else None)

        if not last:
            pc = _pool_call(pvc, pool_w_b[l], pool_scale[l], False)
            hc = _merge_ffn_call(ofc, obc, pc, hc, mod_ctx, *mix, *ffn2)
    return h
```

```python
import functools

import numpy as np
import jax
import jax.numpy as jnp
from jax import lax
from jax.experimental import pallas as pl
from jax.experimental.pallas import tpu as pltpu

EPS = 1e-6
N_MOD = 9
HEADS = 4
HEAD_DIM = 128
HG_WIDTH = HEADS * HEAD_DIM
POOL_WINDOWS = (2, 4, 8, 16)
POOL_GD = 128
POOL_WIDTH = POOL_GD * len(POOL_WINDOWS)
GRID_W = 64
CHUNK = 64
SUBLANES = 8
GROUPS = CHUNK // SUBLANES
SCAN_LEVELS = 6
SCAN_UNROLL = 8
PAIR = 2 * HEAD_DIM
POOL_PAD = 8
SUBTILE = 256
LOG2_E = 1.4426950408889634

VMEM_LIMIT_BYTES = 56 * 1024 * 1024

F32 = jnp.float32
BF16 = jnp.bfloat16


def _params(*sem):
    return pltpu.CompilerParams(dimension_semantics=sem, vmem_limit_bytes=VMEM_LIMIT_BYTES)


def _const_spec(shape):
    nd = len(shape)
    return pl.BlockSpec(shape, lambda *_: (0,) * nd, pipeline_mode=pl.Buffered(1))


def _sigmoid(x):
    return jax.nn.sigmoid(x)


def _dot(a, b):
    return jnp.dot(a, b, preferred_element_type=F32)


def _dot_nt(a, b):
    return lax.dot_general(a, b, (((1,), (1,)), ((), ())), preferred_element_type=F32)


def _dot_tn(a, b):
    return lax.dot_general(a, b, (((0,), (0,)), ((), ())), preferred_element_type=F32)


def _norm_modulate(x, nw, shift, scale):
    ms = jnp.mean(x * x, axis=-1, keepdims=True)
    y = x * lax.rsqrt(ms + EPS) * nw
    return y * (1.0 + scale) + shift


def _mod_kernel(c_ref, w_ref, b_ref, o_ref):
    c = c_ref[...]
    s = (c * _sigmoid(c)).astype(BF16)
    o_ref[0] = _dot(s, w_ref[0].astype(BF16)) + b_ref[0]


def _mod_call(cvec, ada_w, ada_b):
    depth, d, nd = ada_w.shape
    rows = cvec.shape[0]
    tn = d
    return pl.pallas_call(
        _mod_kernel,
        grid=(depth, nd // tn),
        in_specs=[
            pl.BlockSpec((rows, d), lambda l, j: (0, 0)),
            pl.BlockSpec((1, d, tn), lambda l, j: (l, 0, j)),
            pl.BlockSpec((1, 1, tn), lambda l, j: (l, 0, j)),
        ],
        out_specs=pl.BlockSpec((1, rows, tn), lambda l, j: (l, 0, j)),
        out_shape=jax.ShapeDtypeStruct((depth, rows, nd), F32),
        compiler_params=_params("parallel", "parallel"),
        name="adaln_mod",
    )(cvec, ada_w, ada_b.reshape(depth, 1, nd))


def _mod_rows(mod_ref, k0):
    return mod_ref[0, k0:k0 + 1, :], mod_ref[0, k0 + 1:k0 + 2, :], mod_ref[0, k0 + 2:k0 + 3, :]


def _ffn_rows(x, mod_ref, k0, nw_ref, wg_ref, wu_ref, wd_ref):
    shift, scale, gate = _mod_rows(mod_ref, k0)
    u = _norm_modulate(x, nw_ref[...], shift, scale).astype(BF16)
    g = _dot(u, wg_ref[...])
    up = _dot(u, wu_ref[...])
    a = (g * _sigmoid(g) * up).astype(BF16)
    return x + (0.5 * gate) * _dot(a, wd_ref[...])


def _subtiles(tm):
    sub = min(tm, SUBTILE)
    return [slice(r0, r0 + sub) for r0 in range(0, tm, sub)]


def _mod_spec(mod):
    nd = mod.shape[-1]
    if mod.shape[0] == 1:
        return pl.BlockSpec((1, N_MOD, nd), lambda b, i: (0, 0, 0))
    return pl.BlockSpec((1, N_MOD, nd), lambda b, i: (b, 0, 0))


def _row_tile(n, target):
    return min(n, target)


def _chunk_cumsum(x, reverse):
    n = x.shape[-1]
    x3 = x.reshape(GROUPS, SUBLANES, n)
    r8 = lax.broadcasted_iota(jnp.int32, (1, SUBLANES, n), 1)
    s = 1
    while s < SUBLANES:
        if reverse:
            x3 = x3 + jnp.where(r8 < SUBLANES - s, pltpu.roll(x3, SUBLANES - s, 1), 0.0)
        else:
            x3 = x3 + jnp.where(r8 >= s, pltpu.roll(x3, s, 1), 0.0)
        s *= 2
    edge = 0 if reverse else SUBLANES - 1
    totals = [x3[g, edge:edge + 1, :] for g in range(GROUPS)]
    order = range(GROUPS - 1, -1, -1) if reverse else range(GROUPS)
    out = [None] * GROUPS
    run = None
    for g in order:
        out[g] = x3[g] if run is None else x3[g] + run
        run = totals[g] if run is None else run + totals[g]
    return jnp.concatenate(out, axis=0)


def _inproj_epilogue(z_all, z_pool, rows, lb_f, lb_b, q_o, kf_o, cf_o, kb_o, cb_o, v_o, pv_o):
    hw = HG_WIDTH

    def proj(c0, width):
        return z_all[:, c0:c0 + width]

    q = proj(0, hw)
    q_o[0, rows, :] = (q * _sigmoid(q)).astype(q_o.dtype)

    def gates(z, lb, k_o, c_o, reverse):
        f = lb + (1.0 - lb) * _sigmoid(z)
        k_o[0, rows, :] = (1.0 - f).astype(k_o.dtype)
        lf = jnp.log(f) * LOG2_E
        for r0 in range(0, z_all.shape[0], CHUNK):
            c_o[0, rows.start + r0:rows.start + r0 + CHUNK, :] = _chunk_cumsum(lf[r0:r0 + CHUNK], reverse)

    gates(proj(hw, hw), lb_f, kf_o, cf_o, False)
    gates(proj(2 * hw, hw), lb_b, kb_o, cb_o, True)
    v_o[0, rows, :] = proj(3 * hw, hw).astype(BF16)
    pv_o[0, rows, :] = z_pool


def _ffn_inproj_kernel(h_ref, mod_ref, nw0_ref, wg_ref, wu_ref, wd_ref, nw1_ref, wscan_ref, wpool_ref,
                       lbf_ref, lbb_ref, h_o, *proj_outs):
    for rows in _subtiles(h_ref.shape[1]):
        h1 = _ffn_rows(h_ref[0, rows, :], mod_ref, 0, nw0_ref, wg_ref, wu_ref, wd_ref)
        h_o[0, rows, :] = h1
        u = _norm_modulate(h1, nw1_ref[...], mod_ref[0, 3:4, :], mod_ref[0, 4:5, :]).astype(BF16)
        _inproj_epilogue(_dot(u, wscan_ref[...]), _dot(u, wpool_ref[...]), rows, lbf_ref[...], lbb_ref[...],
                         *proj_outs)


def _col_block_spec(rows, width, start):
    index, rem = divmod(start, width)
    assert rem == 0, (start, width)
    return pl.BlockSpec((rows, width), lambda *_: (0, index), pipeline_mode=pl.Buffered(1))


def _ffn_inproj_call(h, mod, nw0, wg, wu, wd, nw1, w_in, lb_f, lb_b):
    b, n, d = h.shape
    f = wg.shape[1]
    tm = _row_tile(n, 512)
    hw, pw = HG_WIDTH, POOL_WIDTH
    assert w_in.shape[1] == 5 * hw + pw + 2 * d

    def tok(width):
        return pl.BlockSpec((1, tm, width), lambda bi, i: (bi, i, 0))

    def out(width, dtype=F32):
        return jax.ShapeDtypeStruct((b, n, width), dtype)

    widths = [d, hw, hw, hw, hw, hw, hw, pw]
    dtypes = [F32, BF16, BF16, F32, BF16, F32, BF16, F32]
    return pl.pallas_call(
        _ffn_inproj_kernel,
        grid=(b, n // tm),
        in_specs=[tok(d), _mod_spec(mod), _const_spec((1, d)), _const_spec((d, f)), _const_spec((d, f)),
                  _const_spec((f, d)), _const_spec((1, d)), _col_block_spec(d, 4 * hw, 0),
                  _col_block_spec(d, pw, 5 * hw), _const_spec((1, hw)), _const_spec((1, hw))],
        out_specs=[tok(w) for w in widths],
        out_shape=[out(w, t) for w, t in zip(widths, dtypes)],
        compiler_params=_params("parallel", "parallel"),
        name="ffn_in_proj",
    )(h, mod, nw0.reshape(1, d), wg, wu, wd, nw1.reshape(1, d), w_in, w_in, lb_f.reshape(1, hw),
      lb_b.reshape(1, hw))


def _level_masks():
    t = np.arange(CHUNK)[:, None]
    s = np.arange(CHUNK)[None, :]
    out = np.zeros((2, SCAN_LEVELS + 1, CHUNK, CHUNK), np.float32)
    for rev in (0, 1):
        for level in range(SCAN_LEVELS):
            bs = 1 << level
            same = (t >> (level + 1)) == (s >> (level + 1))
            tq = ((t & bs) == 0) if rev else ((t & bs) != 0)
            sk = ((s & bs) != 0) if rev else ((s & bs) == 0)
            out[rev, level] = same & tq & sk
        out[rev, SCAN_LEVELS] = t == s
    return np.concatenate([out, out], axis=-1)


def _neg_abs(x):
    bits = lax.bitcast_convert_type(x, jnp.uint32) | jnp.uint32(0x80000000)
    return lax.bitcast_convert_type(bits, F32)


def _adjacent_products(q3, k3, c3, reverse):
    r8 = lax.broadcasted_iota(jnp.int32, (1, SUBLANES, q3.shape[-1]), 1)
    is_q = ((r8 & 1) == 0) if reverse else ((r8 & 1) != 0)
    shift = SUBLANES - 1 if reverse else 1
    expo = jnp.where(is_q, c3 - pltpu.roll(c3, shift, 1), 0.0)
    return q3 * pltpu.roll(k3, shift, 1) * jnp.exp2(expo)


def _level_operand(q3, k3, c3, level, reverse):
    width = q3.shape[-1]
    bs = 1 << level
    r8 = lax.broadcasted_iota(jnp.int32, (1, SUBLANES, width), 1)
    if bs < SUBLANES:
        is_q = ((r8 & bs) == 0) if reverse else ((r8 & bs) != 0)
        sel = jnp.where(is_q, q3, k3)
        off = bs if reverse else bs - 1
        anchor = c3[:, SUBLANES - 2 * bs + off:SUBLANES - 2 * bs + off + 1, :]
        for blk in range(SUBLANES // (2 * bs) - 2, -1, -1):
            row = blk * 2 * bs + off
            anchor = jnp.where(r8 < (blk + 1) * 2 * bs, c3[:, row:row + 1, :], anchor)
        return sel * jnp.exp2(_neg_abs(c3 - anchor))
    gbs = bs // SUBLANES
    sel, anchor = [], []
    for g in range(GROUPS):
        start = (g // (2 * gbs)) * (2 * gbs)
        is_q = ((g & gbs) == 0) if reverse else ((g & gbs) != 0)
        sel.append((q3 if is_q else k3)[g:g + 1])
        if reverse:
            anchor.append(c3[start + gbs:start + gbs + 1, 0:1, :])
        else:
            anchor.append(c3[start + gbs - 1:start + gbs, SUBLANES - 1:SUBLANES, :])
    sel = jnp.concatenate(sel, axis=0)
    anchor = jnp.concatenate(anchor, axis=0)
    return sel * jnp.exp2(_neg_abs(c3 - anchor))


def _block_diag(a, b):
    z = jnp.zeros_like(a)
    return jnp.concatenate([jnp.concatenate([a, z], axis=1), jnp.concatenate([z, b], axis=1)], axis=0)


def _scan_direction(q, k, c, v, mask_ref, st_ref, reverse):
    width = q.shape[-1]
    q, k = q.astype(F32), k.astype(F32)
    q3, k3, c3 = (a.reshape(GROUPS, SUBLANES, width) for a in (q, k, c))
    pairs = [(slice(lo, lo + HEAD_DIM), slice(lo + HEAD_DIM, lo + PAIR), slice(lo, lo + PAIR))
             for lo in range(0, width, PAIR)]

    def scores(lhs, rhs, level):
        return [_dot_nt(lhs[:, pr], _block_diag(rhs[:, ha], rhs[:, hb])) * mask_ref[level]
                for ha, hb, pr in pairs]

    lane = lax.broadcasted_iota(jnp.int32, (CHUNK, HEAD_DIM), 1)

    def placed(prod, level):
        return [jnp.where(lane < CHUNK, jnp.sum(prod[:, ha], axis=-1, keepdims=True),
                          jnp.sum(prod[:, hb], axis=-1, keepdims=True)) * mask_ref[level]
                for ha, hb, _ in pairs]

    adjacent = _adjacent_products(q3, k3, c3, reverse).reshape(CHUNK, width)
    accs = [d + a for d, a in zip(placed(q * k, SCAN_LEVELS), placed(adjacent, 0))]
    for level in range(1, SCAN_LEVELS):
        x = _level_operand(q3, k3, c3, level, reverse).reshape(CHUNK, width).astype(BF16)
        accs = [a + s for a, s in zip(accs, scores(x, x, level))]

    last = c[0:1, :] if reverse else c[CHUNK - 1:CHUNK, :]
    qe = (q * jnp.exp2(c)).astype(BF16)
    k_dec = (k * jnp.exp2(last - c)).astype(BF16)
    decay = jnp.exp2(last)
    outs = []
    for p, (ha, hb, _) in enumerate(pairs):
        intra = _dot(accs[p].astype(BF16), _block_diag(v[:, ha], v[:, hb]))
        inter = []
        for h, lanes in ((2 * p, ha), (2 * p + 1, hb)):
            st = st_ref[h]
            inter.append(_dot_nt(qe[:, lanes], st.astype(BF16)))
            st_ref[h] = st * decay[:, lanes] + _dot_tn(v[:, lanes], k_dec[:, lanes])
        outs.append(intra + jnp.concatenate(inter, axis=1))
    return jnp.concatenate(outs, axis=1)


def _scan_kernel(qf_ref, vf_ref, kf_ref, cf_ref, qb_ref, vb_ref, kb_ref, cb_ref, s0_ref, mask_ref,
                 of_ref, ob_ref, sfin_ref, st_ref, *, n_chunks):
    j = pl.program_id(1)

    @pl.when(j == 0)
    def _():
        st_ref[...] = s0_ref[0]

    def body(i, carry):
        rf = pl.ds(pl.multiple_of(i * CHUNK, CHUNK), CHUNK)
        of_ref[0, rf, :] = _scan_direction(qf_ref[0, rf, :], kf_ref[0, rf, :], cf_ref[0, rf, :],
                                           vf_ref[0, rf, :], mask_ref.at[0], st_ref.at[0], False)
        rb = pl.ds(pl.multiple_of((n_chunks - 1 - i) * CHUNK, CHUNK), CHUNK)
        ob_ref[0, rb, :] = _scan_direction(qb_ref[0, rb, :], kb_ref[0, rb, :], cb_ref[0, rb, :],
                                           vb_ref[0, rb, :], mask_ref.at[1], st_ref.at[1], True)
        return carry

    lax.fori_loop(0, n_chunks, body, 0, unroll=min(n_chunks, SCAN_UNROLL))

    @pl.when(j == pl.num_programs(1) - 1)
    def _():
        sfin_ref[0] = st_ref[...]


def _scan_call(q, kf, cf, kb, cb, v, s0):
    b, n, hw = q.shape
    ts = _row_tile(n, 1024)
    nb = n // ts
    fwd = pl.BlockSpec((1, ts, hw), lambda bi, j: (bi, j, 0))
    bwd = pl.BlockSpec((1, ts, hw), lambda bi, j: (bi, nb - 1 - j, 0))
    st_shape = (2, HEADS, HEAD_DIM, HEAD_DIM)
    st_spec = pl.BlockSpec((1,) + st_shape, lambda bi, j: (bi, 0, 0, 0, 0))
    masks = jnp.asarray(_level_masks())
    return pl.pallas_call(
        functools.partial(_scan_kernel, n_chunks=ts // CHUNK),
        grid=(b, nb),
        in_specs=[fwd, fwd, fwd, fwd, bwd, bwd, bwd, bwd, st_spec, _const_spec(masks.shape)],
        out_specs=[fwd, bwd, st_spec],
        out_shape=[jax.ShapeDtypeStruct((b, n, hw), F32), jax.ShapeDtypeStruct((b, n, hw), F32),
                   jax.ShapeDtypeStruct((b,) + st_shape, F32)],
        scratch_shapes=[pltpu.VMEM(st_shape, F32)],
        compiler_params=_params("parallel", "arbitrary"),
        name="gla_scan",
    )(q, v, kf, cf, q, v, kb, cb, s0, masks)


def _inv_count(shape, axis, length, w):
    i = lax.broadcasted_iota(jnp.int32, shape, axis) - POOL_PAD
    hi = jnp.minimum(i + (w - w // 2), length)
    lo = jnp.maximum(i - w // 2, 0)
    return 1.0 / jnp.maximum(hi - lo, 1).astype(F32)


def _pool_body(x_ref, pw_ref, sc_ref, o_ref, pad_ref, w, two_d):
    rows, width, gd = x_ref.shape[1:]
    wp = width + 2 * POOL_PAD
    x = x_ref[0]
    t = x
    if two_d:
        z = jnp.zeros((POOL_PAD, width, gd), F32)
        t = jnp.concatenate([z, x, z], axis=0)
        n = t.shape[0]
        t = t[0:n - 1] + t[1:n]
        first = 1
        k = 2
        while k < w:
            n = t.shape[0]
            t = t[0:n - k] + t[k:n]
            first += k // 2
            k *= 2
        t = t[POOL_PAD - first:POOL_PAD - first + rows]
        t = t * _inv_count((rows + 2 * POOL_PAD, 1, gd), 0, rows, w)[POOL_PAD:POOL_PAD + rows]
    halo = jnp.zeros((rows, POOL_PAD, gd), F32)
    pad_ref[:, 0:POOL_PAD, :] = halo
    pad_ref[:, POOL_PAD + width:wp, :] = halo
    pad_ref[:, POOL_PAD:POOL_PAD + width, :] = t
    t = pad_ref[...]
    k = 1
    while k < w:
        t = t + pltpu.roll(t, k, 1)
        k *= 2
    if w > 2:
        t = pltpu.roll(t, wp - (w // 2 - 1), 1)
    t = t * _inv_count((1, wp, gd), 1, width, w)
    m = t[:, POOL_PAD:POOL_PAD + width, :]
    dlt = (m - x).reshape(rows * width, gd).astype(BF16)
    y = _dot(dlt, pw_ref[0])
    o_ref[0] = (y * sc_ref[...]).astype(o_ref.dtype)


def _pool_kernel(x_ref, pw_ref, sc_ref, o_ref, pad_ref, *, two_d):
    g = pl.program_id(0)
    for gi, w in enumerate(POOL_WINDOWS):
        @pl.when(g == gi)
        def _(w=w):
            _pool_body(x_ref, pw_ref, sc_ref, o_ref, pad_ref, w, two_d)


def _pool_call(pv, pool_w, pool_scale, two_d):
    b, n, pw = pv.shape
    width = GRID_W if two_d else n
    rows = n // width
    groups = len(POOL_WINDOWS)
    return pl.pallas_call(
        functools.partial(_pool_kernel, two_d=two_d),
        grid=(groups, b),
        in_specs=[
            pl.BlockSpec((1, rows, width, POOL_GD), lambda g, bi: (bi, 0, 0, g)),
            pl.BlockSpec((1, POOL_GD, POOL_GD), lambda g, bi: (g, 0, 0)),
            pl.BlockSpec((1, POOL_GD), lambda g, bi: (0, g)),
        ],
        out_specs=pl.BlockSpec((1, n, POOL_GD), lambda g, bi: (bi, 0, g)),
        out_shape=jax.ShapeDtypeStruct((b, n, pw), BF16),
        scratch_shapes=[pltpu.VMEM((rows, width + 2 * POOL_PAD, POOL_GD), F32)],
        compiler_params=_params("parallel", "parallel"),
        name="pool_mixer",
    )(pv.reshape(b, rows, width, pw), pool_w, pool_scale.reshape(1, pw))


def _merge_rows(x, o, p, mod_ref, nw_ref, hnw_ref, wg_ref, wga_ref, wgb_ref, wa_ref, wb_ref, wo_ref):
    u = _norm_modulate(x, nw_ref[...], mod_ref[0, 3:4, :], mod_ref[0, 4:5, :]).astype(BF16)
    parts = []
    for h in range(HEADS):
        oh = o[:, h * HEAD_DIM:(h + 1) * HEAD_DIM]
        ms = jnp.mean(oh * oh, axis=-1, keepdims=True)
        parts.append(oh * lax.rsqrt(ms + EPS))
    on = jnp.concatenate(parts, axis=-1) * hnw_ref[...]
    g = _dot(u, wg_ref[...])
    a = (on * (g * _sigmoid(g))).astype(BF16)
    ga = _sigmoid(_dot(u, wga_ref[...]))
    gb = _sigmoid(_dot(u, wgb_ref[...]))
    merged = ga * _dot(a, wa_ref[...]) + gb * _dot(p, wb_ref[...])
    y = _dot(merged.astype(BF16), wo_ref[...])
    return x + mod_ref[0, 5:6, :] * y


def _merge_ffn_kernel(*refs, final):
    (of_ref, ob_ref, p_ref, h_ref, mod_ref, nw1_ref, hnw_ref, wog_ref, wga_ref, wgb_ref, wa_ref, wb_ref,
     wo_ref, nw2_ref, wg_ref, wu_ref, wd_ref) = refs[:17]
    fnw_ref = refs[17] if final else None
    o_ref = refs[-1]
    for rows in _subtiles(h_ref.shape[1]):
        o = of_ref[0, rows, :] + ob_ref[0, rows, :]
        hm = _merge_rows(h_ref[0, rows, :], o, p_ref[0, rows, :], mod_ref, nw1_ref, hnw_ref,
                         wog_ref, wga_ref, wgb_ref, wa_ref, wb_ref, wo_ref)
        out = _ffn_rows(hm, mod_ref, 6, nw2_ref, wg_ref, wu_ref, wd_ref)
        if final:
            ms = jnp.mean(out * out, axis=-1, keepdims=True)
            out = out * lax.rsqrt(ms + EPS) * fnw_ref[...]
        o_ref[0, rows, :] = out


def _merge_ffn_call(o_f, o_b, p, h, mod, nw1, hnw, w_in, w_a, w_b, w_out, nw2, wg, wu, wd, final_nw=None):
    b, n, d = h.shape
    f = wg.shape[1]
    hw, pw = HG_WIDTH, POOL_WIDTH
    tm = _row_tile(n, 512)
    final = final_nw is not None

    def tok(width):
        return pl.BlockSpec((1, tm, width), lambda bi, i: (bi, i, 0))

    in_specs = [tok(hw), tok(hw), tok(pw), tok(d), _mod_spec(mod), _const_spec((1, d)),
                _const_spec((1, hw)),
                _col_block_spec(d, hw, 4 * hw), _col_block_spec(d, d, 5 * hw + pw),
                _col_block_spec(d, d, 5 * hw + pw + d),
                _const_spec((hw, d)), _const_spec((pw, d)), _const_spec((d, d)),
                _const_spec((1, d)), _const_spec((d, f)), _const_spec((d, f)), _const_spec((f, d))]
    args = [o_f, o_b, p, h, mod, nw1.reshape(1, d), hnw.reshape(1, hw), w_in, w_in, w_in, w_a, w_b, w_out,
            nw2.reshape(1, d), wg, wu, wd]
    if final:
        in_specs.append(_const_spec((1, d)))
        args.append(final_nw.reshape(1, d))
    return pl.pallas_call(
        functools.partial(_merge_ffn_kernel, final=final),
        grid=(b, n // tm),
        in_specs=in_specs,
        out_specs=tok(d),
        out_shape=jax.ShapeDtypeStruct((b, n, d), F32),
        compiler_params=_params("parallel", "parallel"),
        name="merge_ffn",
    )(*args)


def kernel(x, c, ctx, c_ctx, ada_w, ada_b, norm_w, ffn_wg, ffn_wu, ffn_wd, w_in, lower_bounds, hg_norm_w,
           pool_w, pool_scale, w_branch_a, w_branch_b, w_out, final_norm_w):
    b, n, d = x.shape
    depth = ada_w.shape[0]
    assert n % GRID_W == 0 and n % CHUNK == 0 and ctx.shape[1] % CHUNK == 0

    n_rows = -(-(b + 1) // 8) * 8
    cvec = jnp.zeros((n_rows, d), F32).at[:b].set(c).at[b].set(c_ctx)
    mods = _mod_call(cvec, ada_w, ada_b).reshape(depth, n_rows, N_MOD, d)

    sm = jax.nn.softmax(lower_bounds.astype(F32), axis=1)
    lbs = jnp.cumsum(sm, axis=1) - sm[:, :1]

    wg, wu, wd = ffn_wg.astype(BF16), ffn_wu.astype(BF16), ffn_wd.astype(BF16)
    w_in_b = w_in.astype(BF16)
    w_a, w_b, w_o = w_branch_a.astype(BF16), w_branch_b.astype(BF16), w_out.astype(BF16)
    pool_w_b = pool_w.astype(BF16)

    zeros_state = jnp.zeros((b, 2, HEADS, HEAD_DIM, HEAD_DIM), F32)
    h, hc = x, ctx
    for l in range(depth):
        last = l == depth - 1
        mod_lat, mod_ctx = mods[l, :b], mods[l, b:b + 1]
        ffn1 = (norm_w[l, 0], wg[l, 0], wu[l, 0], wd[l, 0])
        ffn2 = (norm_w[l, 2], wg[l, 1], wu[l, 1], wd[l, 1])
        proj = (norm_w[l, 1], w_in_b[l], lbs[0, l], lbs[1, l])
        mix = (norm_w[l, 1], hg_norm_w[l], w_in_b[l], w_a[l], w_b[l], w_o[l])

        hc, qc, kfc, cfc, kbc, cbc, vc, pvc = _ffn_inproj_call(hc, mod_ctx, *ffn1, *proj)
        ofc, obc, s_ctx = _scan_call(qc, kfc, cfc, kbc, cbc, vc, zeros_state)

        h, q, kf, cf, kb, cb, v, pv = _ffn_inproj_call(h, mod_lat, *ffn1, *proj)
        o_f, o_b, _ = _scan_call(q, kf, cf, kb, cb, v, s_ctx)
        p = _pool_call(pv, pool_w_b[l], pool_scale[l], True)
        h = _merge_ffn_call(o_f, o_b, p, h, mod_lat, *mix, *ffn2, final_nw=final_norm_w if last else None)

        if not last:
            pc = _pool_call(pvc, pool_w_b[l], pool_scale[l], False)
            hc = _merge_ffn_call(ofc, obc, pc, hc, mod_ctx, *mix, *ffn2)
    return h
```

```python
import functools

import numpy as np
import jax
import jax.numpy as jnp
from jax import lax
from jax.experimental import pallas as pl
from jax.experimental.pallas import tpu as pltpu

EPS = 1e-6
N_MOD = 9
HEADS = 4
HEAD_DIM = 128
HG_WIDTH = HEADS * HEAD_DIM
POOL_WINDOWS = (2, 4, 8, 16)
POOL_GD = 128
POOL_WIDTH = POOL_GD * len(POOL_WINDOWS)
GRID_W = 64
CHUNK = 64
SUBLANES = 8
GROUPS = CHUNK // SUBLANES
SCAN_LEVELS = 6
SCAN_UNROLL = 8
PAIR = 2 * HEAD_DIM
POOL_PAD = 8
SUBTILE = 256
LOG2_E = 1.4426950408889634

VMEM_LIMIT_BYTES = 56 * 1024 * 1024

F32 = jnp.float32
BF16 = jnp.bfloat16


def _params(*sem):
    return pltpu.CompilerParams(dimension_semantics=sem, vmem_limit_bytes=VMEM_LIMIT_BYTES)


def _const_spec(shape):
    nd = len(shape)
    return pl.BlockSpec(shape, lambda *_: (0,) * nd, pipeline_mode=pl.Buffered(1))


def _sigmoid(x):
    return jax.nn.sigmoid(x)


def _dot(a, b):
    return jnp.dot(a, b, preferred_element_type=F32)


def _dot_nt(a, b):
    return lax.dot_general(a, b, (((1,), (1,)), ((), ())), preferred_element_type=F32)


def _dot_tn(a, b):
    return lax.dot_general(a, b, (((0,), (0,)), ((), ())), preferred_element_type=F32)


def _norm_modulate(x, nw, shift, scale):
    ms = jnp.mean(x * x, axis=-1, keepdims=True)
    y = x * lax.rsqrt(ms + EPS) * nw
    return y * (1.0 + scale) + shift


def _mod_kernel(c_ref, w_ref, b_ref, o_ref):
    c = c_ref[...]
    s = (c * _sigmoid(c)).astype(BF16)
    o_ref[0] = _dot(s, w_ref[0].astype(BF16)) + b_ref[0]


def _mod_call(cvec, ada_w, ada_b):
    depth, d, nd = ada_w.shape
    rows = cvec.shape[0]
    tn = d
    return pl.pallas_call(
        _mod_kernel,
        grid=(depth, nd // tn),
        in_specs=[
            pl.BlockSpec((rows, d), lambda l, j: (0, 0)),
            pl.BlockSpec((1, d, tn), lambda l, j: (l, 0, j)),
            pl.BlockSpec((1, 1, tn), lambda l, j: (l, 0, j)),
        ],
        out_specs=pl.BlockSpec((1, rows, tn), lambda l, j: (l, 0, j)),
        out_shape=jax.ShapeDtypeStruct((depth, rows, nd), F32),
        compiler_params=_params("parallel", "parallel"),
        name="adaln_mod",
    )(cvec, ada_w, ada_b.reshape(depth, 1, nd))


def _mod_rows(mod_ref, k0):
    return mod_ref[0, k0:k0 + 1, :], mod_ref[0, k0 + 1:k0 + 2, :], mod_ref[0, k0 + 2:k0 + 3, :]


def _ffn_rows(x, mod_ref, k0, nw_ref, wg_ref, wu_ref, wd_ref):
    shift, scale, gate = _mod_rows(mod_ref, k0)
    u = _norm_modulate(x, nw_ref[...], shift, scale).astype(BF16)
    g = _dot(u, wg_ref[...])
    up = _dot(u, wu_ref[...])
    a = (g * _sigmoid(g) * up).astype(BF16)
    return x + (0.5 * gate) * _dot(a, wd_ref[...])


def _subtiles(tm):
    sub = min(tm, SUBTILE)
    return [slice(r0, r0 + sub) for r0 in range(0, tm, sub)]


def _mod_spec(mod):
    nd = mod.shape[-1]
    if mod.shape[0] == 1:
        return pl.BlockSpec((1, N_MOD, nd), lambda b, i: (0, 0, 0))
    return pl.BlockSpec((1, N_MOD, nd), lambda b, i: (b, 0, 0))


def _row_tile(n, target):
    return min(n, target)


def _chunk_cumsum(x, reverse):
    n = x.shape[-1]
    x3 = x.reshape(GROUPS, SUBLANES, n)
    r8 = lax.broadcasted_iota(jnp.int32, (1, SUBLANES, n), 1)
    s = 1
    while s < SUBLANES:
        if reverse:
            x3 = x3 + jnp.where(r8 < SUBLANES - s, pltpu.roll(x3, SUBLANES - s, 1), 0.0)
        else:
            x3 = x3 + jnp.where(r8 >= s, pltpu.roll(x3, s, 1), 0.0)
        s *= 2
    edge = 0 if reverse else SUBLANES - 1
    totals = [x3[g, edge:edge + 1, :] for g in range(GROUPS)]
    order = range(GROUPS - 1, -1, -1) if reverse else range(GROUPS)
    out = [None] * GROUPS
    run = None
    for g in order:
        out[g] = x3[g] if run is None else x3[g] + run
        run = totals[g] if run is None else run + totals[g]
    return jnp.concatenate(out, axis=0)


def _inproj_epilogue(z_all, z_pool, rows, lb_f, lb_b, q_o, kf_o, cf_o, kb_o, cb_o, v_o, pv_o):
    hw = HG_WIDTH

    def proj(c0, width):
        return z_all[:, c0:c0 + width]

    q = proj(0, hw)
    q_o[0, rows, :] = (q * _sigmoid(q)).astype(q_o.dtype)

    def gates(z, lb, k_o, c_o, reverse):
        f = lb + (1.0 - lb) * _sigmoid(z)
        k_o[0, rows, :] = (1.0 - f).astype(k_o.dtype)
        lf = jnp.log(f) * LOG2_E
        for r0 in range(0, z_all.shape[0], CHUNK):
            c_o[0, rows.start + r0:rows.start + r0 + CHUNK, :] = _chunk_cumsum(lf[r0:r0 + CHUNK], reverse)

    gates(proj(hw, hw), lb_f, kf_o, cf_o, False)
    gates(proj(2 * hw, hw), lb_b, kb_o, cb_o, True)
    v_o[0, rows, :] = proj(3 * hw, hw).astype(BF16)
    pv_o[0, rows, :] = z_pool


def _ffn_inproj_kernel(h_ref, mod_ref, nw0_ref, wg_ref, wu_ref, wd_ref, nw1_ref, w_ref,
                       lbf_ref, lbb_ref, h_o, *proj_outs):
    n_scan = 4 * HG_WIDTH
    for rows in _subtiles(h_ref.shape[1]):
        h1 = _ffn_rows(h_ref[0, rows, :], mod_ref, 0, nw0_ref, wg_ref, wu_ref, wd_ref)
        h_o[0, rows, :] = h1
        u = _norm_modulate(h1, nw1_ref[...], mod_ref[0, 3:4, :], mod_ref[0, 4:5, :]).astype(BF16)
        z = _dot(u, w_ref[...])
        _inproj_epilogue(z[:, :n_scan], z[:, n_scan:], rows, lbf_ref[...], lbb_ref[...], *proj_outs)


def _col_block_spec(rows, width, start):
    index, rem = divmod(start, width)
    assert rem == 0, (start, width)
    return pl.BlockSpec((rows, width), lambda *_: (0, index), pipeline_mode=pl.Buffered(1))


def _ffn_inproj_call(h, mod, nw0, wg, wu, wd, nw1, w_main, lb_f, lb_b):
    b, n, d = h.shape
    f = wg.shape[1]
    tm = _row_tile(n, 512)
    hw, pw = HG_WIDTH, POOL_WIDTH
    assert w_main.shape[1] == 4 * hw + pw

    def tok(width):
        return pl.BlockSpec((1, tm, width), lambda bi, i: (bi, i, 0))

    def out(width, dtype=F32):
        return jax.ShapeDtypeStruct((b, n, width), dtype)

    widths = [d, hw, hw, hw, hw, hw, hw, pw]
    dtypes = [F32, BF16, BF16, F32, BF16, F32, BF16, F32]
    return pl.pallas_call(
        _ffn_inproj_kernel,
        grid=(b, n // tm),
        in_specs=[tok(d), _mod_spec(mod), _const_spec((1, d)), _const_spec((d, f)), _const_spec((d, f)),
                  _const_spec((f, d)), _const_spec((1, d)), _const_spec(w_main.shape),
                  _const_spec((1, hw)), _const_spec((1, hw))],
        out_specs=[tok(w) for w in widths],
        out_shape=[out(w, t) for w, t in zip(widths, dtypes)],
        compiler_params=_params("parallel", "parallel"),
        name="ffn_in_proj",
    )(h, mod, nw0.reshape(1, d), wg, wu, wd, nw1.reshape(1, d), w_main, lb_f.reshape(1, hw),
      lb_b.reshape(1, hw))


def _level_masks():
    t = np.arange(CHUNK)[:, None]
    s = np.arange(CHUNK)[None, :]
    out = np.zeros((2, SCAN_LEVELS + 1, CHUNK, CHUNK), np.float32)
    for rev in (0, 1):
        for level in range(SCAN_LEVELS):
            bs = 1 << level
            same = (t >> (level + 1)) == (s >> (level + 1))
            tq = ((t & bs) == 0) if rev else ((t & bs) != 0)
            sk = ((s & bs) != 0) if rev else ((s & bs) == 0)
            out[rev, level] = same & tq & sk
        out[rev, SCAN_LEVELS] = t == s
    return np.concatenate([out, out], axis=-1)


def _neg_abs(x):
    bits = lax.bitcast_convert_type(x, jnp.uint32) | jnp.uint32(0x80000000)
    return lax.bitcast_convert_type(bits, F32)


def _adjacent_products(q3, k3, c3, reverse):
    r8 = lax.broadcasted_iota(jnp.int32, (1, SUBLANES, q3.shape[-1]), 1)
    is_q = ((r8 & 1) == 0) if reverse else ((r8 & 1) != 0)
    shift = SUBLANES - 1 if reverse else 1
    expo = jnp.where(is_q, c3 - pltpu.roll(c3, shift, 1), 0.0)
    return q3 * pltpu.roll(k3, shift, 1) * jnp.exp2(expo)


def _level_operand(q3, k3, c3, level, reverse):
    width = q3.shape[-1]
    bs = 1 << level
    r8 = lax.broadcasted_iota(jnp.int32, (1, SUBLANES, width), 1)
    if bs < SUBLANES:
        is_q = ((r8 & bs) == 0) if reverse else ((r8 & bs) != 0)
        sel = jnp.where(is_q, q3, k3)
        off = bs if reverse else bs - 1
        anchor = c3[:, SUBLANES - 2 * bs + off:SUBLANES - 2 * bs + off + 1, :]
        for blk in range(SUBLANES // (2 * bs) - 2, -1, -1):
            row = blk * 2 * bs + off
            anchor = jnp.where(r8 < (blk + 1) * 2 * bs, c3[:, row:row + 1, :], anchor)
        return sel * jnp.exp2(_neg_abs(c3 - anchor))
    gbs = bs // SUBLANES
    sel, anchor = [], []
    for g in range(GROUPS):
        start = (g // (2 * gbs)) * (2 * gbs)
        is_q = ((g & gbs) == 0) if reverse else ((g & gbs) != 0)
        sel.append((q3 if is_q else k3)[g:g + 1])
        if reverse:
            anchor.append(c3[start + gbs:start + gbs + 1, 0:1, :])
        else:
            anchor.append(c3[start + gbs - 1:start + gbs, SUBLANES - 1:SUBLANES, :])
    sel = jnp.concatenate(sel, axis=0)
    anchor = jnp.concatenate(anchor, axis=0)
    return sel * jnp.exp2(_neg_abs(c3 - anchor))


def _block_diag(a, b):
    z = jnp.zeros_like(a)
    return jnp.concatenate([jnp.concatenate([a, z], axis=1), jnp.concatenate([z, b], axis=1)], axis=0)


def _scan_direction(q, k, c, v, mask_ref, st_ref, reverse):
    width = q.shape[-1]
    q, k = q.astype(F32), k.astype(F32)
    q3, k3, c3 = (a.reshape(GROUPS, SUBLANES, width) for a in (q, k, c))
    pairs = [(slice(lo, lo + HEAD_DIM), slice(lo + HEAD_DIM, lo + PAIR), slice(lo, lo + PAIR))
             for lo in range(0, width, PAIR)]

    def scores(lhs, rhs, level):
        return [_dot_nt(lhs[:, pr], _block_diag(rhs[:, ha], rhs[:, hb])) * mask_ref[level]
                for ha, hb, pr in pairs]

    lane = lax.broadcasted_iota(jnp.int32, (CHUNK, HEAD_DIM), 1)

    def placed(prod, level):
        return [jnp.where(lane < CHUNK, jnp.sum(prod[:, ha], axis=-1, keepdims=True),
                          jnp.sum(prod[:, hb], axis=-1, keepdims=True)) * mask_ref[level]
                for ha, hb, _ in pairs]

    adjacent = _adjacent_products(q3, k3, c3, reverse).reshape(CHUNK, width)
    accs = [d + a for d, a in zip(placed(q * k, SCAN_LEVELS), placed(adjacent, 0))]
    for level in range(1, SCAN_LEVELS):
        x = _level_operand(q3, k3, c3, level, reverse).reshape(CHUNK, width).astype(BF16)
        accs = [a + s for a, s in zip(accs, scores(x, x, level))]

    last = c[0:1, :] if reverse else c[CHUNK - 1:CHUNK, :]
    qe = (q * jnp.exp2(c)).astype(BF16)
    k_dec = (k * jnp.exp2(last - c)).astype(BF16)
    decay = jnp.exp2(last)
    outs = []
    for p, (ha, hb, _) in enumerate(pairs):
        intra = _dot(accs[p].astype(BF16), _block_diag(v[:, ha], v[:, hb]))
        inter = []
        for h, lanes in ((2 * p, ha), (2 * p + 1, hb)):
            st = st_ref[h]
            inter.append(_dot_nt(qe[:, lanes], st.astype(BF16)))
            st_ref[h] = st * decay[:, lanes] + _dot_tn(v[:, lanes], k_dec[:, lanes])
        outs.append(intra + jnp.concatenate(inter, axis=1))
    return jnp.concatenate(outs, axis=1)


def _scan_kernel(qf_ref, vf_ref, kf_ref, cf_ref, qb_ref, vb_ref, kb_ref, cb_ref, s0_ref, mask_ref,
                 of_ref, ob_ref, sfin_ref, st_ref, *, n_chunks):
    j = pl.program_id(1)

    @pl.when(j == 0)
    def _():
        st_ref[...] = s0_ref[0]

    def body(i, carry):
        rf = pl.ds(pl.multiple_of(i * CHUNK, CHUNK), CHUNK)
        of_ref[0, rf, :] = _scan_direction(qf_ref[0, rf, :], kf_ref[0, rf, :], cf_ref[0, rf, :],
                                           vf_ref[0, rf, :], mask_ref.at[0], st_ref.at[0], False)
        rb = pl.ds(pl.multiple_of((n_chunks - 1 - i) * CHUNK, CHUNK), CHUNK)
        ob_ref[0, rb, :] = _scan_direction(qb_ref[0, rb, :], kb_ref[0, rb, :], cb_ref[0, rb, :],
                                           vb_ref[0, rb, :], mask_ref.at[1], st_ref.at[1], True)
        return carry

    lax.fori_loop(0, n_chunks, body, 0, unroll=min(n_chunks, SCAN_UNROLL))

    @pl.when(j == pl.num_programs(1) - 1)
    def _():
        sfin_ref[0] = st_ref[...]


def _scan_call(q, kf, cf, kb, cb, v, s0):
    b, n, hw = q.shape
    ts = _row_tile(n, 1024)
    nb = n // ts
    fwd = pl.BlockSpec((1, ts, hw), lambda bi, j: (bi, j, 0))
    bwd = pl.BlockSpec((1, ts, hw), lambda bi, j: (bi, nb - 1 - j, 0))
    st_shape = (2, HEADS, HEAD_DIM, HEAD_DIM)
    st_spec = pl.BlockSpec((1,) + st_shape, lambda bi, j: (bi, 0, 0, 0, 0))
    masks = jnp.asarray(_level_masks())
    return pl.pallas_call(
        functools.partial(_scan_kernel, n_chunks=ts // CHUNK),
        grid=(b, nb),
        in_specs=[fwd, fwd, fwd, fwd, bwd, bwd, bwd, bwd, st_spec, _const_spec(masks.shape)],
        out_specs=[fwd, bwd, st_spec],
        out_shape=[jax.ShapeDtypeStruct((b, n, hw), F32), jax.ShapeDtypeStruct((b, n, hw), F32),
                   jax.ShapeDtypeStruct((b,) + st_shape, F32)],
        scratch_shapes=[pltpu.VMEM(st_shape, F32)],
        compiler_params=_params("parallel", "arbitrary"),
        name="gla_scan",
    )(q, v, kf, cf, q, v, kb, cb, s0, masks)


def _inv_count(shape, axis, length, w):
    i = lax.broadcasted_iota(jnp.int32, shape, axis) - POOL_PAD
    hi = jnp.minimum(i + (w - w // 2), length)
    lo = jnp.maximum(i - w // 2, 0)
    return 1.0 / jnp.maximum(hi - lo, 1).astype(F32)


def _pool_body(x_ref, pw_ref, sc_ref, o_ref, pad_ref, w, two_d):
    rows, width, gd = x_ref.shape[1:]
    wp = width + 2 * POOL_PAD
    x = x_ref[0]
    t = x
    if two_d:
        z = jnp.zeros((POOL_PAD, width, gd), F32)
        t = jnp.concatenate([z, x, z], axis=0)
        n = t.shape[0]
        t = t[0:n - 1] + t[1:n]
        first = 1
        k = 2
        while k < w:
            n = t.shape[0]
            t = t[0:n - k] + t[k:n]
            first += k // 2
            k *= 2
        t = t[POOL_PAD - first:POOL_PAD - first + rows]
        t = t * _inv_count((rows + 2 * POOL_PAD, 1, gd), 0, rows, w)[POOL_PAD:POOL_PAD + rows]
    halo = jnp.zeros((rows, POOL_PAD, gd), F32)
    pad_ref[:, 0:POOL_PAD, :] = halo
    pad_ref[:, POOL_PAD + width:wp, :] = halo
    pad_ref[:, POOL_PAD:POOL_PAD + width, :] = t
    t = pad_ref[...]
    k = 1
    while k < w:
        t = t + pltpu.roll(t, k, 1)
        k *= 2
    if w > 2:
        t = pltpu.roll(t, wp - (w // 2 - 1), 1)
    t = t * _inv_count((1, wp, gd), 1, width, w)
    m = t[:, POOL_PAD:POOL_PAD + width, :]
    dlt = (m - x).reshape(rows * width, gd).astype(BF16)
    y = _dot(dlt, pw_ref[0])
    o_ref[0] = (y * sc_ref[...]).astype(o_ref.dtype)


def _pool_kernel(x_ref, pw_ref, sc_ref, o_ref, pad_ref, *, two_d):
    g = pl.program_id(0)
    for gi, w in enumerate(POOL_WINDOWS):
        @pl.when(g == gi)
        def _(w=w):
            _pool_body(x_ref, pw_ref, sc_ref, o_ref, pad_ref, w, two_d)


def _pool_call(pv, pool_w, pool_scale, two_d):
    b, n, pw = pv.shape
    width = GRID_W if two_d else n
    rows = n // width
    groups = len(POOL_WINDOWS)
    return pl.pallas_call(
        functools.partial(_pool_kernel, two_d=two_d),
        grid=(groups, b),
        in_specs=[
            pl.BlockSpec((1, rows, width, POOL_GD), lambda g, bi: (bi, 0, 0, g)),
            pl.BlockSpec((1, POOL_GD, POOL_GD), lambda g, bi: (g, 0, 0)),
            pl.BlockSpec((1, POOL_GD), lambda g, bi: (0, g)),
        ],
        out_specs=pl.BlockSpec((1, n, POOL_GD), lambda g, bi: (bi, 0, g)),
        out_shape=jax.ShapeDtypeStruct((b, n, pw), BF16),
        scratch_shapes=[pltpu.VMEM((rows, width + 2 * POOL_PAD, POOL_GD), F32)],
        compiler_params=_params("parallel", "parallel"),
        name="pool_mixer",
    )(pv.reshape(b, rows, width, pw), pool_w, pool_scale.reshape(1, pw))


def _merge_rows(x, o, p, mod_ref, nw_ref, hnw_ref, wg_ref, wga_ref, wgb_ref, wa_ref, wb_ref, wo_ref):
    u = _norm_modulate(x, nw_ref[...], mod_ref[0, 3:4, :], mod_ref[0, 4:5, :]).astype(BF16)
    parts = []
    for h in range(HEADS):
        oh = o[:, h * HEAD_DIM:(h + 1) * HEAD_DIM]
        ms = jnp.mean(oh * oh, axis=-1, keepdims=True)
        parts.append(oh * lax.rsqrt(ms + EPS))
    on = jnp.concatenate(parts, axis=-1) * hnw_ref[...]
    g = _dot(u, wg_ref[...])
    a = (on * (g * _sigmoid(g))).astype(BF16)
    ga = _sigmoid(_dot(u, wga_ref[...]))
    gb = _sigmoid(_dot(u, wgb_ref[...]))
    merged = ga * _dot(a, wa_ref[...]) + gb * _dot(p, wb_ref[...])
    y = _dot(merged.astype(BF16), wo_ref[...])
    return x + mod_ref[0, 5:6, :] * y


def _merge_ffn_kernel(*refs, final):
    (of_ref, ob_ref, p_ref, h_ref, mod_ref, nw1_ref, hnw_ref, wog_ref, wga_ref, wgb_ref, wa_ref, wb_ref,
     wo_ref, nw2_ref, wg_ref, wu_ref, wd_ref) = refs[:17]
    fnw_ref = refs[17] if final else None
    o_ref = refs[-1]
    for rows in _subtiles(h_ref.shape[1]):
        o = of_ref[0, rows, :] + ob_ref[0, rows, :]
        hm = _merge_rows(h_ref[0, rows, :], o, p_ref[0, rows, :], mod_ref, nw1_ref, hnw_ref,
                         wog_ref, wga_ref, wgb_ref, wa_ref, wb_ref, wo_ref)
        out = _ffn_rows(hm, mod_ref, 6, nw2_ref, wg_ref, wu_ref, wd_ref)
        if final:
            ms = jnp.mean(out * out, axis=-1, keepdims=True)
            out = out * lax.rsqrt(ms + EPS) * fnw_ref[...]
        o_ref[0, rows, :] = out


def _merge_ffn_call(o_f, o_b, p, h, mod, nw1, hnw, w_in, w_a, w_b, w_out, nw2, wg, wu, wd, final_nw=None):
    b, n, d = h.shape
    f = wg.shape[1]
    hw, pw = HG_WIDTH, POOL_WIDTH
    tm = _row_tile(n, 512)
    final = final_nw is not None

    def tok(width):
        return pl.BlockSpec((1, tm, width), lambda bi, i: (bi, i, 0))

    in_specs = [tok(hw), tok(hw), tok(pw), tok(d), _mod_spec(mod), _const_spec((1, d)),
                _const_spec((1, hw)),
                _col_block_spec(d, hw, 4 * hw), _col_block_spec(d, d, 5 * hw + pw),
                _col_block_spec(d, d, 5 * hw + pw + d),
                _const_spec((hw, d)), _const_spec((pw, d)), _const_spec((d, d)),
                _const_spec((1, d)), _const_spec((d, f)), _const_spec((d, f)), _const_spec((f, d))]
    args = [o_f, o_b, p, h, mod, nw1.reshape(1, d), hnw.reshape(1, hw), w_in, w_in, w_in, w_a, w_b, w_out,
            nw2.reshape(1, d), wg, wu, wd]
    if final:
        in_specs.append(_const_spec((1, d)))
        args.append(final_nw.reshape(1, d))
    return pl.pallas_call(
        functools.partial(_merge_ffn_kernel, final=final),
        grid=(b, n // tm),
        in_specs=in_specs,
        out_specs=tok(d),
        out_shape=jax.ShapeDtypeStruct((b, n, d), F32),
        compiler_params=_params("parallel", "parallel"),
        name="merge_ffn",
    )(*args)


def kernel(x, c, ctx, c_ctx, ada_w, ada_b, norm_w, ffn_wg, ffn_wu, ffn_wd, w_in, lower_bounds, hg_norm_w,
           pool_w, pool_scale, w_branch_a, w_branch_b, w_out, final_norm_w):
    b, n, d = x.shape
    depth = ada_w.shape[0]
    assert n % GRID_W == 0 and n % CHUNK == 0 and ctx.shape[1] % CHUNK == 0

    n_rows = -(-(b + 1) // 8) * 8
    cvec = jnp.zeros((n_rows, d), F32).at[:b].set(c).at[b].set(c_ctx)
    mods = _mod_call(cvec, ada_w, ada_b).reshape(depth, n_rows, N_MOD, d)

    sm = jax.nn.softmax(lower_bounds.astype(F32), axis=1)
    lbs = jnp.cumsum(sm, axis=1) - sm[:, :1]

    def bf16(w):
        return w.astype(BF16)

    hw, pw = HG_WIDTH, POOL_WIDTH
    zeros_state = jnp.zeros((b, 2, HEADS, HEAD_DIM, HEAD_DIM), F32)
    h, hc = x, ctx
    for l in range(depth):
        last = l == depth - 1
        mod_lat, mod_ctx = mods[l, :b], mods[l, b:b + 1]
        ffn1 = (norm_w[l, 0], bf16(ffn_wg[l, 0]), bf16(ffn_wu[l, 0]), bf16(ffn_wd[l, 0]))
        ffn2 = (norm_w[l, 2], bf16(ffn_wg[l, 1]), bf16(ffn_wu[l, 1]), bf16(ffn_wd[l, 1]))
        w_in_b = bf16(w_in[l])
        w_main = bf16(jnp.concatenate([w_in[l, :, :4 * hw], w_in[l, :, 5 * hw:5 * hw + pw]], axis=-1))
        proj = (norm_w[l, 1], w_main, lbs[0, l], lbs[1, l])
        mix = (norm_w[l, 1], hg_norm_w[l], w_in_b, bf16(w_branch_a[l]), bf16(w_branch_b[l]), bf16(w_out[l]))
        pool_w_b = bf16(pool_w[l])

        hc, qc, kfc, cfc, kbc, cbc, vc, pvc = _ffn_inproj_call(hc, mod_ctx, *ffn1, *proj)
        ofc, obc, s_ctx = _scan_call(qc, kfc, cfc, kbc, cbc, vc, zeros_state)

        h, q, kf, cf, kb, cb, v, pv = _ffn_inproj_call(h, mod_lat, *ffn1, *proj)
        o_f, o_b, _ = _scan_call(q, kf, cf, kb, cb, v, s_ctx)
        p = _pool_call(pv, pool_w_b, pool_scale[l], True)
        h = _merge_ffn_call(o_f, o_b, p, h, mod_lat, *mix, *ffn2, final_nw=final_norm_w if last else None)

        if not last:
            pc = _pool_call(pvc, pool_w_b, pool_scale[l], False)
            hc = _merge_ffn_call(ofc, obc, pc, hc, mod_ctx, *mix, *ffn2)
    return h
```

```python
import functools

import numpy as np
import jax
import jax.numpy as jnp
from jax import lax
from jax.experimental import pallas as pl
from jax.experimental.pallas import tpu as pltpu

EPS = 1e-6
N_MOD = 9
HEADS = 4
HEAD_DIM = 128
HG_WIDTH = HEADS * HEAD_DIM
POOL_WINDOWS = (2, 4, 8, 16)
POOL_GD = 128
POOL_WIDTH = POOL_GD * len(POOL_WINDOWS)
GRID_W = 64
CHUNK = 64
SUBLANES = 8
GROUPS = CHUNK // SUBLANES
SCAN_LEVELS = 6
SCAN_UNROLL = 8
PAIR = 2 * HEAD_DIM
POOL_PAD = 8
SUBTILE = 256
LOG2_E = 1.4426950408889634

VMEM_LIMIT_BYTES = 56 * 1024 * 1024

F32 = jnp.float32
BF16 = jnp.bfloat16


def _params(*sem):
    return pltpu.CompilerParams(dimension_semantics=sem, vmem_limit_bytes=VMEM_LIMIT_BYTES)


def _const_spec(shape):
    nd = len(shape)
    return pl.BlockSpec(shape, lambda *_: (0,) * nd, pipeline_mode=pl.Buffered(1))


def _stacked_spec(w):
    arr, idx = w
    tail = arr.shape[len(idx):]
    return pl.BlockSpec((None,) * len(idx) + tail, lambda *_: idx + (0,) * len(tail),
                        pipeline_mode=pl.Buffered(1))


def _sigmoid(x):
    return jax.nn.sigmoid(x)


def _dot(a, b):
    return jnp.dot(a, b, preferred_element_type=F32)


def _dot_nt(a, b):
    return lax.dot_general(a, b, (((1,), (1,)), ((), ())), preferred_element_type=F32)


def _dot_tn(a, b):
    return lax.dot_general(a, b, (((0,), (0,)), ((), ())), preferred_element_type=F32)


def _norm_modulate(x, nw, shift, scale):
    ms = jnp.mean(x * x, axis=-1, keepdims=True)
    y = x * lax.rsqrt(ms + EPS) * nw
    return y * (1.0 + scale) + shift


def _mod_kernel(c_ref, w_ref, b_ref, o_ref):
    c = c_ref[...]
    s = (c * _sigmoid(c)).astype(BF16)
    o_ref[0] = _dot(s, w_ref[0].astype(BF16)) + b_ref[0]


def _mod_call(cvec, ada_w, ada_b):
    depth, d, nd = ada_w.shape
    rows = cvec.shape[0]
    tn = d
    return pl.pallas_call(
        _mod_kernel,
        grid=(depth, nd // tn),
        in_specs=[
            pl.BlockSpec((rows, d), lambda l, j: (0, 0)),
            pl.BlockSpec((1, d, tn), lambda l, j: (l, 0, j)),
            pl.BlockSpec((1, 1, tn), lambda l, j: (l, 0, j)),
        ],
        out_specs=pl.BlockSpec((1, rows, tn), lambda l, j: (l, 0, j)),
        out_shape=jax.ShapeDtypeStruct((depth, rows, nd), F32),
        compiler_params=_params("parallel", "parallel"),
        name="adaln_mod",
    )(cvec, ada_w, ada_b.reshape(depth, 1, nd))


def _mod_rows(mod_ref, k0):
    return mod_ref[0, k0:k0 + 1, :], mod_ref[0, k0 + 1:k0 + 2, :], mod_ref[0, k0 + 2:k0 + 3, :]


def _ffn_rows(x, mod_ref, k0, nw_ref, wg_ref, wu_ref, wd_ref):
    shift, scale, gate = _mod_rows(mod_ref, k0)
    u = _norm_modulate(x, nw_ref[...], shift, scale).astype(BF16)
    g = _dot(u, wg_ref[...])
    up = _dot(u, wu_ref[...])
    a = (g * _sigmoid(g) * up).astype(BF16)
    return x + (0.5 * gate) * _dot(a, wd_ref[...])


def _subtiles(tm):
    sub = min(tm, SUBTILE)
    return [slice(r0, r0 + sub) for r0 in range(0, tm, sub)]


def _mod_spec(mod):
    nd = mod.shape[-1]
    if mod.shape[0] == 1:
        return pl.BlockSpec((1, N_MOD, nd), lambda b, i: (0, 0, 0))
    return pl.BlockSpec((1, N_MOD, nd), lambda b, i: (b, 0, 0))


def _row_tile(n, target):
    return min(n, target)


def _chunk_cumsum(x, reverse):
    n = x.shape[-1]
    x3 = x.reshape(GROUPS, SUBLANES, n)
    r8 = lax.broadcasted_iota(jnp.int32, (1, SUBLANES, n), 1)
    s = 1
    while s < SUBLANES:
        if reverse:
            x3 = x3 + jnp.where(r8 < SUBLANES - s, pltpu.roll(x3, SUBLANES - s, 1), 0.0)
        else:
            x3 = x3 + jnp.where(r8 >= s, pltpu.roll(x3, s, 1), 0.0)
        s *= 2
    edge = 0 if reverse else SUBLANES - 1
    totals = [x3[g, edge:edge + 1, :] for g in range(GROUPS)]
    order = range(GROUPS - 1, -1, -1) if reverse else range(GROUPS)
    out = [None] * GROUPS
    run = None
    for g in order:
        out[g] = x3[g] if run is None else x3[g] + run
        run = totals[g] if run is None else run + totals[g]
    return jnp.concatenate(out, axis=0)


def _inproj_epilogue(z_all, z_pool, rows, lb_f, lb_b, q_o, kf_o, cf_o, kb_o, cb_o, v_o, pv_o):
    hw = HG_WIDTH

    def proj(c0, width):
        return z_all[:, c0:c0 + width]

    q = proj(0, hw)
    q_o[0, rows, :] = (q * _sigmoid(q)).astype(q_o.dtype)

    def gates(z, lb, k_o, c_o, reverse):
        f = lb + (1.0 - lb) * _sigmoid(z)
        k_o[0, rows, :] = (1.0 - f).astype(k_o.dtype)
        lf = jnp.log(f) * LOG2_E
        for r0 in range(0, z_all.shape[0], CHUNK):
            c_o[0, rows.start + r0:rows.start + r0 + CHUNK, :] = _chunk_cumsum(lf[r0:r0 + CHUNK], reverse)

    gates(proj(hw, hw), lb_f, kf_o, cf_o, False)
    gates(proj(2 * hw, hw), lb_b, kb_o, cb_o, True)
    v_o[0, rows, :] = proj(3 * hw, hw).astype(BF16)
    pv_o[0, rows, :] = z_pool


def _ffn_inproj_kernel(h_ref, mod_ref, nw0_ref, wg_ref, wu_ref, wd_ref, nw1_ref, w_ref,
                       lbf_ref, lbb_ref, h_o, *proj_outs):
    n_scan = 4 * HG_WIDTH
    for rows in _subtiles(h_ref.shape[1]):
        h1 = _ffn_rows(h_ref[0, rows, :], mod_ref, 0, nw0_ref, wg_ref, wu_ref, wd_ref)
        h_o[0, rows, :] = h1
        u = _norm_modulate(h1, nw1_ref[...], mod_ref[0, 3:4, :], mod_ref[0, 4:5, :]).astype(BF16)
        z = _dot(u, w_ref[...])
        _inproj_epilogue(z[:, :n_scan], z[:, n_scan:], rows, lbf_ref[...], lbb_ref[...], *proj_outs)


def _col_block_spec(rows, width, start):
    index, rem = divmod(start, width)
    assert rem == 0, (start, width)
    return pl.BlockSpec((rows, width), lambda *_: (0, index), pipeline_mode=pl.Buffered(1))


def _ffn_inproj_call(h, mod, nw0, wg, wu, wd, nw1, w_main, lb_f, lb_b):
    b, n, d = h.shape
    f = wg[0].shape[-1]
    tm = _row_tile(n, 512)
    hw, pw = HG_WIDTH, POOL_WIDTH
    assert w_main.shape[1] == 4 * hw + pw

    def tok(width):
        return pl.BlockSpec((1, tm, width), lambda bi, i: (bi, i, 0))

    def out(width, dtype=F32):
        return jax.ShapeDtypeStruct((b, n, width), dtype)

    widths = [d, hw, hw, hw, hw, hw, hw, pw]
    dtypes = [F32, BF16, BF16, F32, BF16, F32, BF16, F32]
    return pl.pallas_call(
        _ffn_inproj_kernel,
        grid=(b, n // tm),
        in_specs=[tok(d), _mod_spec(mod), _const_spec((1, d)), _stacked_spec(wg), _stacked_spec(wu),
                  _stacked_spec(wd), _const_spec((1, d)), _const_spec(w_main.shape),
                  _const_spec((1, hw)), _const_spec((1, hw))],
        out_specs=[tok(w) for w in widths],
        out_shape=[out(w, t) for w, t in zip(widths, dtypes)],
        compiler_params=_params("parallel", "parallel"),
        name="ffn_in_proj",
    )(h, mod, nw0.reshape(1, d), wg[0], wu[0], wd[0], nw1.reshape(1, d), w_main, lb_f.reshape(1, hw),
      lb_b.reshape(1, hw))


def _level_masks():
    t = np.arange(CHUNK)[:, None]
    s = np.arange(CHUNK)[None, :]
    out = np.zeros((2, SCAN_LEVELS + 1, CHUNK, CHUNK), np.float32)
    for rev in (0, 1):
        for level in range(SCAN_LEVELS):
            bs = 1 << level
            same = (t >> (level + 1)) == (s >> (level + 1))
            tq = ((t & bs) == 0) if rev else ((t & bs) != 0)
            sk = ((s & bs) != 0) if rev else ((s & bs) == 0)
            out[rev, level] = same & tq & sk
        out[rev, SCAN_LEVELS] = t == s
    return np.concatenate([out, out], axis=-1)


def _neg_abs(x):
    bits = lax.bitcast_convert_type(x, jnp.uint32) | jnp.uint32(0x80000000)
    return lax.bitcast_convert_type(bits, F32)


def _adjacent_products(q3, k3, c3, reverse):
    r8 = lax.broadcasted_iota(jnp.int32, (1, SUBLANES, q3.shape[-1]), 1)
    is_q = ((r8 & 1) == 0) if reverse else ((r8 & 1) != 0)
    shift = SUBLANES - 1 if reverse else 1
    expo = jnp.where(is_q, c3 - pltpu.roll(c3, shift, 1), 0.0)
    return q3 * pltpu.roll(k3, shift, 1) * jnp.exp2(expo)


def _level_operand(q3, k3, c3, level, reverse):
    width = q3.shape[-1]
    bs = 1 << level
    r8 = lax.broadcasted_iota(jnp.int32, (1, SUBLANES, width), 1)
    if bs < SUBLANES:
        is_q = ((r8 & bs) == 0) if reverse else ((r8 & bs) != 0)
        sel = jnp.where(is_q, q3, k3)
        off = bs if reverse else bs - 1
        anchor = c3[:, SUBLANES - 2 * bs + off:SUBLANES - 2 * bs + off + 1, :]
        for blk in range(SUBLANES // (2 * bs) - 2, -1, -1):
            row = blk * 2 * bs + off
            anchor = jnp.where(r8 < (blk + 1) * 2 * bs, c3[:, row:row + 1, :], anchor)
        return sel * jnp.exp2(_neg_abs(c3 - anchor))
    gbs = bs // SUBLANES
    sel, anchor = [], []
    for g in range(GROUPS):
        start = (g // (2 * gbs)) * (2 * gbs)
        is_q = ((g & gbs) == 0) if reverse else ((g & gbs) != 0)
        sel.append((q3 if is_q else k3)[g:g + 1])
        if reverse:
            anchor.append(c3[start + gbs:start + gbs + 1, 0:1, :])
        else:
            anchor.append(c3[start + gbs - 1:start + gbs, SUBLANES - 1:SUBLANES, :])
    sel = jnp.concatenate(sel, axis=0)
    anchor = jnp.concatenate(anchor, axis=0)
    return sel * jnp.exp2(_neg_abs(c3 - anchor))


def _block_diag(a, b):
    z = jnp.zeros_like(a)
    return jnp.concatenate([jnp.concatenate([a, z], axis=1), jnp.concatenate([z, b], axis=1)], axis=0)


def _scan_direction(q, k, c, v, mask_ref, st_ref, reverse):
    width = q.shape[-1]
    q, k = q.astype(F32), k.astype(F32)
    q3, k3, c3 = (a.reshape(GROUPS, SUBLANES, width) for a in (q, k, c))
    pairs = [(slice(lo, lo + HEAD_DIM), slice(lo + HEAD_DIM, lo + PAIR), slice(lo, lo + PAIR))
             for lo in range(0, width, PAIR)]

    def scores(lhs, rhs, level):
        return [_dot_nt(lhs[:, pr], _block_diag(rhs[:, ha], rhs[:, hb])) * mask_ref[level]
                for ha, hb, pr in pairs]

    lane = lax.broadcasted_iota(jnp.int32, (CHUNK, HEAD_DIM), 1)

    def placed(prod, level):
        return [jnp.where(lane < CHUNK, jnp.sum(prod[:, ha], axis=-1, keepdims=True),
                          jnp.sum(prod[:, hb], axis=-1, keepdims=True)) * mask_ref[level]
                for ha, hb, _ in pairs]

    adjacent = _adjacent_products(q3, k3, c3, reverse).reshape(CHUNK, width)
    accs = [d + a for d, a in zip(placed(q * k, SCAN_LEVELS), placed(adjacent, 0))]
    for level in range(1, SCAN_LEVELS):
        x = _level_operand(q3, k3, c3, level, reverse).reshape(CHUNK, width).astype(BF16)
        accs = [a + s for a, s in zip(accs, scores(x, x, level))]

    last = c[0:1, :] if reverse else c[CHUNK - 1:CHUNK, :]
    qe = (q * jnp.exp2(c)).astype(BF16)
    k_dec = (k * jnp.exp2(last - c)).astype(BF16)
    decay = jnp.exp2(last)
    outs = []
    for p, (ha, hb, _) in enumerate(pairs):
        intra = _dot(accs[p].astype(BF16), _block_diag(v[:, ha], v[:, hb]))
        inter = []
        for h, lanes in ((2 * p, ha), (2 * p + 1, hb)):
            st = st_ref[h]
            inter.append(_dot_nt(qe[:, lanes], st.astype(BF16)))
            st_ref[h] = st * decay[:, lanes] + _dot_tn(v[:, lanes], k_dec[:, lanes])
        outs.append(intra + jnp.concatenate(inter, axis=1))
    return jnp.concatenate(outs, axis=1)


def _scan_kernel(qf_ref, vf_ref, kf_ref, cf_ref, qb_ref, vb_ref, kb_ref, cb_ref, s0_ref, mask_ref,
                 of_ref, ob_ref, sfin_ref, st_ref, *, n_chunks):
    j = pl.program_id(1)

    @pl.when(j == 0)
    def _():
        st_ref[...] = s0_ref[0]

    def body(i, carry):
        rf = pl.ds(pl.multiple_of(i * CHUNK, CHUNK), CHUNK)
        of_ref[0, rf, :] = _scan_direction(qf_ref[0, rf, :], kf_ref[0, rf, :], cf_ref[0, rf, :],
                                           vf_ref[0, rf, :], mask_ref.at[0], st_ref.at[0], False)
        rb = pl.ds(pl.multiple_of((n_chunks - 1 - i) * CHUNK, CHUNK), CHUNK)
        ob_ref[0, rb, :] = _scan_direction(qb_ref[0, rb, :], kb_ref[0, rb, :], cb_ref[0, rb, :],
                                           vb_ref[0, rb, :], mask_ref.at[1], st_ref.at[1], True)
        return carry

    lax.fori_loop(0, n_chunks, body, 0, unroll=min(n_chunks, SCAN_UNROLL))

    @pl.when(j == pl.num_programs(1) - 1)
    def _():
        sfin_ref[0] = st_ref[...]


def _scan_call(q, kf, cf, kb, cb, v, s0):
    b, n, hw = q.shape
    ts = _row_tile(n, 1024)
    nb = n // ts
    fwd = pl.BlockSpec((1, ts, hw), lambda bi, j: (bi, j, 0))
    bwd = pl.BlockSpec((1, ts, hw), lambda bi, j: (bi, nb - 1 - j, 0))
    st_shape = (2, HEADS, HEAD_DIM, HEAD_DIM)
    st_spec = pl.BlockSpec((1,) + st_shape, lambda bi, j: (bi, 0, 0, 0, 0))
    masks = jnp.asarray(_level_masks())
    return pl.pallas_call(
        functools.partial(_scan_kernel, n_chunks=ts // CHUNK),
        grid=(b, nb),
        in_specs=[fwd, fwd, fwd, fwd, bwd, bwd, bwd, bwd, st_spec, _const_spec(masks.shape)],
        out_specs=[fwd, bwd, st_spec],
        out_shape=[jax.ShapeDtypeStruct((b, n, hw), F32), jax.ShapeDtypeStruct((b, n, hw), F32),
                   jax.ShapeDtypeStruct((b,) + st_shape, F32)],
        scratch_shapes=[pltpu.VMEM(st_shape, F32)],
        compiler_params=_params("parallel", "arbitrary"),
        name="gla_scan",
    )(q, v, kf, cf, q, v, kb, cb, s0, masks)


def _inv_count(shape, axis, length, w):
    i = lax.broadcasted_iota(jnp.int32, shape, axis) - POOL_PAD
    hi = jnp.minimum(i + (w - w // 2), length)
    lo = jnp.maximum(i - w // 2, 0)
    return 1.0 / jnp.maximum(hi - lo, 1).astype(F32)


def _pool_body(x_ref, pw_ref, sc_ref, o_ref, pad_ref, w, two_d):
    rows, width, gd = x_ref.shape[1:]
    wp = width + 2 * POOL_PAD
    x = x_ref[0]
    t = x
    if two_d:
        z = jnp.zeros((POOL_PAD, width, gd), F32)
        t = jnp.concatenate([z, x, z], axis=0)
        n = t.shape[0]
        t = t[0:n - 1] + t[1:n]
        first = 1
        k = 2
        while k < w:
            n = t.shape[0]
            t = t[0:n - k] + t[k:n]
            first += k // 2
            k *= 2
        t = t[POOL_PAD - first:POOL_PAD - first + rows]
        t = t * _inv_count((rows + 2 * POOL_PAD, 1, gd), 0, rows, w)[POOL_PAD:POOL_PAD + rows]
    halo = jnp.zeros((rows, POOL_PAD, gd), F32)
    pad_ref[:, 0:POOL_PAD, :] = halo
    pad_ref[:, POOL_PAD + width:wp, :] = halo
    pad_ref[:, POOL_PAD:POOL_PAD + width, :] = t
    t = pad_ref[...]
    k = 1
    while k < w:
        t = t + pltpu.roll(t, k, 1)
        k *= 2
    if w > 2:
        t = pltpu.roll(t, wp - (w // 2 - 1), 1)
    t = t * _inv_count((1, wp, gd), 1, width, w)
    m = t[:, POOL_PAD:POOL_PAD + width, :]
    dlt = (m - x).reshape(rows * width, gd).astype(BF16)
    y = _dot(dlt, pw_ref[0])
    o_ref[0] = (y * sc_ref[...]).astype(o_ref.dtype)


def _pool_kernel(x_ref, pw_ref, sc_ref, o_ref, pad_ref, *, two_d):
    g = pl.program_id(0)
    for gi, w in enumerate(POOL_WINDOWS):
        @pl.when(g == gi)
        def _(w=w):
            _pool_body(x_ref, pw_ref, sc_ref, o_ref, pad_ref, w, two_d)


def _pool_call(pv, pool_w, pool_scale, two_d):
    b, n, pw = pv.shape
    width = GRID_W if two_d else n
    rows = n // width
    groups = len(POOL_WINDOWS)
    return pl.pallas_call(
        functools.partial(_pool_kernel, two_d=two_d),
        grid=(groups, b),
        in_specs=[
            pl.BlockSpec((1, rows, width, POOL_GD), lambda g, bi: (bi, 0, 0, g)),
            pl.BlockSpec((1, POOL_GD, POOL_GD), lambda g, bi: (g, 0, 0)),
            pl.BlockSpec((1, POOL_GD), lambda g, bi: (0, g)),
        ],
        out_specs=pl.BlockSpec((1, n, POOL_GD), lambda g, bi: (bi, 0, g)),
        out_shape=jax.ShapeDtypeStruct((b, n, pw), BF16),
        scratch_shapes=[pltpu.VMEM((rows, width + 2 * POOL_PAD, POOL_GD), F32)],
        compiler_params=_params("parallel", "parallel"),
        name="pool_mixer",
    )(pv.reshape(b, rows, width, pw), pool_w, pool_scale.reshape(1, pw))


def _merge_rows(x, o, p, mod_ref, nw_ref, hnw_ref, wg_ref, wga_ref, wgb_ref, wa_ref, wb_ref, wo_ref):
    u = _norm_modulate(x, nw_ref[...], mod_ref[0, 3:4, :], mod_ref[0, 4:5, :]).astype(BF16)
    parts = []
    for h in range(HEADS):
        oh = o[:, h * HEAD_DIM:(h + 1) * HEAD_DIM]
        ms = jnp.mean(oh * oh, axis=-1, keepdims=True)
        parts.append(oh * lax.rsqrt(ms + EPS))
    on = jnp.concatenate(parts, axis=-1) * hnw_ref[...]
    g = _dot(u, wg_ref[...])
    a = (on * (g * _sigmoid(g))).astype(BF16)
    ga = _sigmoid(_dot(u, wga_ref[...]))
    gb = _sigmoid(_dot(u, wgb_ref[...]))
    merged = ga * _dot(a, wa_ref[...]) + gb * _dot(p, wb_ref[...])
    y = _dot(merged.astype(BF16), wo_ref[...])
    return x + mod_ref[0, 5:6, :] * y


def _merge_ffn_kernel(*refs, final):
    (of_ref, ob_ref, p_ref, h_ref, mod_ref, nw1_ref, hnw_ref, wog_ref, wga_ref, wgb_ref, wa_ref, wb_ref,
     wo_ref, nw2_ref, wg_ref, wu_ref, wd_ref) = refs[:17]
    fnw_ref = refs[17] if final else None
    o_ref = refs[-1]
    for rows in _subtiles(h_ref.shape[1]):
        o = of_ref[0, rows, :] + ob_ref[0, rows, :]
        hm = _merge_rows(h_ref[0, rows, :], o, p_ref[0, rows, :], mod_ref, nw1_ref, hnw_ref,
                         wog_ref, wga_ref, wgb_ref, wa_ref, wb_ref, wo_ref)
        out = _ffn_rows(hm, mod_ref, 6, nw2_ref, wg_ref, wu_ref, wd_ref)
        if final:
            ms = jnp.mean(out * out, axis=-1, keepdims=True)
            out = out * lax.rsqrt(ms + EPS) * fnw_ref[...]
        o_ref[0, rows, :] = out


def _merge_ffn_call(o_f, o_b, p, h, mod, nw1, hnw, w_in, w_a, w_b, w_out, nw2, wg, wu, wd, final_nw=None):
    b, n, d = h.shape
    f = wg[0].shape[-1]
    hw, pw = HG_WIDTH, POOL_WIDTH
    tm = _row_tile(n, 512)
    final = final_nw is not None

    def tok(width):
        return pl.BlockSpec((1, tm, width), lambda bi, i: (bi, i, 0))

    in_specs = [tok(hw), tok(hw), tok(pw), tok(d), _mod_spec(mod), _const_spec((1, d)),
                _const_spec((1, hw)),
                _col_block_spec(d, hw, 4 * hw), _col_block_spec(d, d, 5 * hw + pw),
                _col_block_spec(d, d, 5 * hw + pw + d),
                _const_spec((hw, d)), _const_spec((pw, d)), _const_spec((d, d)),
                _const_spec((1, d)), _stacked_spec(wg), _stacked_spec(wu), _stacked_spec(wd)]
    args = [o_f, o_b, p, h, mod, nw1.reshape(1, d), hnw.reshape(1, hw), w_in, w_in, w_in, w_a, w_b, w_out,
            nw2.reshape(1, d), wg[0], wu[0], wd[0]]
    if final:
        in_specs.append(_const_spec((1, d)))
        args.append(final_nw.reshape(1, d))
    return pl.pallas_call(
        functools.partial(_merge_ffn_kernel, final=final),
        grid=(b, n // tm),
        in_specs=in_specs,
        out_specs=tok(d),
        out_shape=jax.ShapeDtypeStruct((b, n, d), F32),
        compiler_params=_params("parallel", "parallel"),
        name="merge_ffn",
    )(*args)


def kernel(x, c, ctx, c_ctx, ada_w, ada_b, norm_w, ffn_wg, ffn_wu, ffn_wd, w_in, lower_bounds, hg_norm_w,
           pool_w, pool_scale, w_branch_a, w_branch_b, w_out, final_norm_w):
    b, n, d = x.shape
    depth = ada_w.shape[0]
    assert n % GRID_W == 0 and n % CHUNK == 0 and ctx.shape[1] % CHUNK == 0

    n_rows = -(-(b + 1) // 8) * 8
    cvec = jnp.zeros((n_rows, d), F32).at[:b].set(c).at[b].set(c_ctx)
    mods = _mod_call(cvec, ada_w, ada_b).reshape(depth, n_rows, N_MOD, d)

    sm = jax.nn.softmax(lower_bounds.astype(F32), axis=1)
    lbs = jnp.cumsum(sm, axis=1) - sm[:, :1]

    def bf16(w):
        return w.astype(BF16)

    wg_all, wu_all, wd_all = bf16(ffn_wg), bf16(ffn_wu), bf16(ffn_wd)
    hw, pw = HG_WIDTH, POOL_WIDTH
    zeros_state = jnp.zeros((b, 2, HEADS, HEAD_DIM, HEAD_DIM), F32)
    h, hc = x, ctx
    for l in range(depth):
        last = l == depth - 1
        mod_lat, mod_ctx = mods[l, :b], mods[l, b:b + 1]
        ffn1 = (norm_w[l, 0], (wg_all, (l, 0)), (wu_all, (l, 0)), (wd_all, (l, 0)))
        ffn2 = (norm_w[l, 2], (wg_all, (l, 1)), (wu_all, (l, 1)), (wd_all, (l, 1)))
        w_in_b = bf16(w_in[l])
        w_main = bf16(jnp.concatenate([w_in[l, :, :4 * hw], w_in[l, :, 5 * hw:5 * hw + pw]], axis=-1))
        proj = (norm_w[l, 1], w_main, lbs[0, l], lbs[1, l])
        mix = (norm_w[l, 1], hg_norm_w[l], w_in_b, bf16(w_branch_a[l]), bf16(w_branch_b[l]), bf16(w_out[l]))
        pool_w_b = bf16(pool_w[l])

        hc, qc, kfc, cfc, kbc, cbc, vc, pvc = _ffn_inproj_call(hc, mod_ctx, *ffn1, *proj)
        ofc, obc, s_ctx = _scan_call(qc, kfc, cfc, kbc, cbc, vc, zeros_state)

        h, q, kf, cf, kb, cb, v, pv = _ffn_inproj_call(h, mod_lat, *ffn1, *proj)
        o_f, o_b, _ = _scan_call(q, kf, cf, kb, cb, v, s_ctx)
        p = _pool_call(pv, pool_w_b, pool_scale[l], True)
        h = _merge_ffn_call(o_f, o_b, p, h, mod_lat, *mix, *ffn2, final_nw=final_norm_w if last else None)

        if not last:
            pc = _pool_call(pvc, pool_w_b, pool_scale[l], False)
            hc = _merge_ffn_call(ofc, obc, pc, hc, mod_ctx, *mix, *ffn2)
    return h
```
